```python
import math
import jax, jax.numpy as jnp
from jax import lax
import numpy as np

D_MODEL = 1024
BATCH = 4
SEQ = 8192
DEPTH = 2

N_A_LAYERS = DEPTH // 2
N_B_LAYERS = DEPTH - N_A_LAYERS

SSM_WIDTH = D_MODEL
SSM_GROUP = 16
SSM_GROUPS = SSM_WIDTH // SSM_GROUP
SSM_STATE = 64
SSM_CHUNK = 128
DT_MIN = 1e-3
DT_MAX = 1e-1

N_HEADS = 8
HEAD_DIM = 64
V_HEAD_DIM = 2 * HEAD_DIM
QK_WIDTH = N_HEADS * 2 * HEAD_DIM
ATTN_WIDTH = N_HEADS * V_HEAD_DIM
ROT_DIM = HEAD_DIM // 4
ROPE_THETA = 500000.0
Q_BLOCK = 128

NORM_EPS = 1e-6
SUBLN_EPS = 1e-5

kernel_name = "yoco_s5_diffattn_hybrid"


def rmsnorm(x, g, eps=NORM_EPS):
    xf = x.astype(jnp.float32)
    xf = xf * lax.rsqrt(jnp.mean(xf * xf, axis=-1, keepdims=True) + eps)
    return xf.astype(x.dtype) * g


def rope_tables(seq):
    inv = ROPE_THETA ** (-jnp.arange(0, ROT_DIM, 2, dtype=jnp.float32) / ROT_DIM)
    ang = jnp.arange(seq, dtype=jnp.float32)[:, None] * inv[None, :]
    return jnp.cos(ang), jnp.sin(ang)


def partial_rope(x, cos, sin):
    half = ROT_DIM // 2
    x1 = x[..., :half].astype(jnp.float32)
    x2 = x[..., half:ROT_DIM].astype(jnp.float32)
    c = cos[:, None, :]
    s = sin[:, None, :]
    r1 = (x1 * c - x2 * s).astype(x.dtype)
    r2 = (x2 * c + x1 * s).astype(x.dtype)
    return jnp.concatenate([r1, r2, x[..., ROT_DIM:]], axis=-1)


def _complex_affine_combine(e1, e2):
    a1r, a1i, b1r, b1i = e1
    a2r, a2i, b2r, b2i = e2
    return (a2r * a1r - a2i * a1i,
            a2r * a1i + a2i * a1r,
            a2r * b1r - a2i * b1i + b2r,
            a2r * b1i + a2i * b1r + b2i)


def s5_ssm(u, lam_re, lam_im, log_dt, b_re, b_im, c_re, c_im, d_skip):
    bsz, seq, _ = u.shape
    dt = jnp.exp(log_dt)[:, None]
    mag = jnp.exp(lam_re * dt)
    ang = lam_im * dt
    abar_re = mag * jnp.cos(ang)
    abar_im = mag * jnp.sin(ang)
    den = lam_re * lam_re + lam_im * lam_im
    nr = abar_re - 1.0
    ni = abar_im
    f_re = (nr * lam_re + ni * lam_im) / den
    f_im = (ni * lam_re - nr * lam_im) / den
    bbar_re = f_re[..., None] * b_re - f_im[..., None] * b_im
    bbar_im = f_re[..., None] * b_im + f_im[..., None] * b_re

    n_chunks = seq // SSM_CHUNK
    u_c = u.reshape(bsz, n_chunks, SSM_CHUNK, SSM_GROUPS, SSM_GROUP).transpose(1, 2, 0, 3, 4)
    a_re_l = jnp.broadcast_to(abar_re[None, None], (SSM_CHUNK, bsz, SSM_GROUPS, SSM_STATE))
    a_im_l = jnp.broadcast_to(abar_im[None, None], (SSM_CHUNK, bsz, SSM_GROUPS, SSM_STATE))

    def step(carry, uc):
        s_re, s_im = carry
        bu_re = jnp.einsum('lbgh,gph->lbgp', uc, bbar_re)
        bu_im = jnp.einsum('lbgh,gph->lbgp', uc, bbar_im)
        acr, aci, xr, xi = lax.associative_scan(
            _complex_affine_combine, (a_re_l, a_im_l, bu_re, bu_im), axis=0)
        st_re = xr + acr * s_re - aci * s_im
        st_im = xi + acr * s_im + aci * s_re
        y = (jnp.einsum('lbgp,ghp->lbgh', st_re, c_re)
             - jnp.einsum('lbgp,ghp->lbgh', st_im, c_im))
        return (st_re[-1], st_im[-1]), y

    init = (jnp.zeros((bsz, SSM_GROUPS, SSM_STATE), jnp.float32),
            jnp.zeros((bsz, SSM_GROUPS, SSM_STATE), jnp.float32))
    _, y = lax.scan(step, init, u_c)
    y = y.transpose(2, 0, 1, 3, 4).reshape(bsz, seq, SSM_WIDTH)
    return y + d_skip * u


def s5_layer(h, g, in_w, lam_re, lam_im, log_dt, b_re, b_im, c_re, c_im, d_skip,
             glu_w, glu_b, out_w):
    f32 = jnp.float32
    xn = rmsnorm(h, g)
    uz = xn @ in_w
    u, z = jnp.split(uz, [SSM_WIDTH], axis=-1)
    y = s5_ssm(u.astype(f32), lam_re.astype(f32), lam_im.astype(f32), log_dt.astype(f32),
               b_re.astype(f32), b_im.astype(f32), c_re.astype(f32), c_im.astype(f32),
               d_skip.astype(f32)).astype(h.dtype)
    y = jax.nn.gelu(y, approximate=False)
    y = y * jax.nn.sigmoid(y @ glu_w + glu_b)
    y = y * jax.nn.silu(z)
    return y @ out_w


def shared_kv(h, g, kv_w, cos, sin):
    bsz, seq, _ = h.shape
    xn = rmsnorm(h, g)
    kv = xn @ kv_w
    k, v = jnp.split(kv, [QK_WIDTH], axis=-1)
    k = partial_rope(k.reshape(bsz, seq, 2 * N_HEADS, HEAD_DIM), cos, sin)
    k = k.reshape(bsz, seq, N_HEADS, 2, HEAD_DIM)
    k1 = k[..., 0, :].transpose(0, 2, 1, 3)
    k2 = k[..., 1, :].transpose(0, 2, 1, 3)
    v = v.reshape(bsz, seq, N_HEADS, V_HEAD_DIM).transpose(0, 2, 1, 3)
    return k1, k2, v


def diff_attention(q1, q2, k1, k2, v, lam):
    bsz, nh, seq, _ = q1.shape
    nb = seq // Q_BLOCK
    scale = HEAD_DIM ** -0.5
    kpos = jnp.arange(seq)

    def to_blocks(q):
        return q.reshape(bsz, nh, nb, Q_BLOCK, HEAD_DIM).transpose(2, 0, 1, 3, 4)

    def one_block(args):
        q1b, q2b, bi = args
        qpos = bi * Q_BLOCK + jnp.arange(Q_BLOCK)
        mask = kpos[None, :] <= qpos[:, None]

        def probs(qb, kk):
            s = jnp.einsum('bhqd,bhkd->bhqk', qb, kk).astype(jnp.float32) * scale
            return jax.nn.softmax(jnp.where(mask, s, -jnp.inf), axis=-1)

        p = probs(q1b, k1) - lam * probs(q2b, k2)
        return jnp.einsum('bhqk,bhkv->bhqv', p.astype(v.dtype), v)

    o = lax.map(one_block, (to_blocks(q1), to_blocks(q2), jnp.arange(nb)))
    return o.transpose(1, 0, 3, 2, 4).reshape(bsz, seq, nh, V_HEAD_DIM)


def diff_layer(h, g, in_w, lq1, lk1, lq2, lk2, subln_g, out_w, k1, k2, v, cos, sin, lam_init):
    bsz, seq, _ = h.shape
    xn = rmsnorm(h, g)
    qz = xn @ in_w
    q, z = jnp.split(qz, [QK_WIDTH], axis=-1)
    q = partial_rope(q.reshape(bsz, seq, 2 * N_HEADS, HEAD_DIM), cos, sin)
    q = q.reshape(bsz, seq, N_HEADS, 2, HEAD_DIM)
    q1 = q[..., 0, :].transpose(0, 2, 1, 3)
    q2 = q[..., 1, :].transpose(0, 2, 1, 3)
    f32 = jnp.float32
    lam = (jnp.exp(jnp.sum(lq1.astype(f32) * lk1.astype(f32)))
           - jnp.exp(jnp.sum(lq2.astype(f32) * lk2.astype(f32))) + lam_init)
    o = diff_attention(q1, q2, k1, k2, v, lam)
    o = rmsnorm(o, subln_g, SUBLN_EPS) * (1.0 - lam_init)
    o = o.reshape(bsz, seq, ATTN_WIDTH) * jax.nn.silu(z)
    return o @ out_w


def setup_inputs(seed: int = 0) -> dict:
    key = jax.random.key(seed)
    ks = jax.random.split(key, 32)
    f32 = jnp.float32
    nA, nB = N_A_LAYERS, N_B_LAYERS
    G, P, C = SSM_GROUPS, SSM_STATE, SSM_GROUP

    def nrm(k, shape, scale):
        return jax.random.normal(k, shape, f32) * scale

    x = jax.random.normal(ks[0], (BATCH, SEQ, D_MODEL), f32)
    a_norm_g = 1.0 + nrm(ks[1], (nA, D_MODEL), 0.02)
    a_in_w = nrm(ks[2], (nA, D_MODEL, 2 * SSM_WIDTH), D_MODEL ** -0.5)
    n_idx = jnp.arange(P, dtype=f32)
    a_lambda_re = -0.5 + nrm(ks[3], (nA, G, P), 0.01)
    a_lambda_im = jnp.broadcast_to(math.pi * n_idx, (nA, G, P)).astype(f32)
    a_log_dt = jax.random.uniform(ks[4], (nA, G), f32, math.log(DT_MIN), math.log(DT_MAX))
    a_b_re = nrm(ks[5], (nA, G, P, C), (2.0 * C) ** -0.5)
    a_b_im = nrm(ks[6], (nA, G, P, C), (2.0 * C) ** -0.5)
    a_c_re = nrm(ks[7], (nA, G, C, P), (2.0 * P) ** -0.5)
    a_c_im = nrm(ks[8], (nA, G, C, P), (2.0 * P) ** -0.5)
    a_d = nrm(ks[9], (nA, SSM_WIDTH), 1.0)
    a_glu_w = nrm(ks[10], (nA, SSM_WIDTH, SSM_WIDTH), SSM_WIDTH ** -0.5)
    a_glu_b = nrm(ks[11], (nA, SSM_WIDTH), 0.01)
    a_out_w = nrm(ks[12], (nA, SSM_WIDTH, D_MODEL), SSM_WIDTH ** -0.5)
    kv_norm_g = 1.0 + nrm(ks[13], (D_MODEL,), 0.02)
    kv_w = nrm(ks[14], (D_MODEL, QK_WIDTH + ATTN_WIDTH), D_MODEL ** -0.5)
    b_norm_g = 1.0 + nrm(ks[15], (nB, D_MODEL), 0.02)
    b_in_w = nrm(ks[16], (nB, D_MODEL, QK_WIDTH + ATTN_WIDTH), D_MODEL ** -0.5)
    b_lambda_q1 = nrm(ks[17], (nB, HEAD_DIM), 0.1)
    b_lambda_k1 = nrm(ks[18], (nB, HEAD_DIM), 0.1)
    b_lambda_q2 = nrm(ks[19], (nB, HEAD_DIM), 0.1)
    b_lambda_k2 = nrm(ks[20], (nB, HEAD_DIM), 0.1)
    b_subln_g = 1.0 + nrm(ks[21], (nB, V_HEAD_DIM), 0.02)
    b_out_w = nrm(ks[22], (nB, ATTN_WIDTH, D_MODEL), ATTN_WIDTH ** -0.5)
    final_norm_g = 1.0 + nrm(ks[23], (D_MODEL,), 0.02)
    return {"x": x, "a_norm_g": a_norm_g, "a_in_w": a_in_w, "a_lambda_re": a_lambda_re,
            "a_lambda_im": a_lambda_im, "a_log_dt": a_log_dt, "a_b_re": a_b_re, "a_b_im": a_b_im,
            "a_c_re": a_c_re, "a_c_im": a_c_im, "a_d": a_d, "a_glu_w": a_glu_w, "a_glu_b": a_glu_b,
            "a_out_w": a_out_w, "kv_norm_g": kv_norm_g, "kv_w": kv_w, "b_norm_g": b_norm_g,
            "b_in_w": b_in_w, "b_lambda_q1": b_lambda_q1, "b_lambda_k1": b_lambda_k1,
            "b_lambda_q2": b_lambda_q2, "b_lambda_k2": b_lambda_k2, "b_subln_g": b_subln_g,
            "b_out_w": b_out_w, "final_norm_g": final_norm_g}


def reference(x, a_norm_g, a_in_w, a_lambda_re, a_lambda_im, a_log_dt, a_b_re, a_b_im,
              a_c_re, a_c_im, a_d, a_glu_w, a_glu_b, a_out_w, kv_norm_g, kv_w, b_norm_g,
              b_in_w, b_lambda_q1, b_lambda_k1, b_lambda_q2, b_lambda_k2, b_subln_g,
              b_out_w, final_norm_g):
    seq = x.shape[1]
    cos, sin = rope_tables(seq)
    h = x
    k1 = k2 = v = None
    for l in range(DEPTH):
        if l < N_A_LAYERS:
            h = h + s5_layer(h, a_norm_g[l], a_in_w[l], a_lambda_re[l], a_lambda_im[l],
                             a_log_dt[l], a_b_re[l], a_b_im[l], a_c_re[l], a_c_im[l], a_d[l],
                             a_glu_w[l], a_glu_b[l], a_out_w[l])
        else:
            if l == N_A_LAYERS:
                k1, k2, v = shared_kv(h, kv_norm_g, kv_w, cos, sin)
            j = l - N_A_LAYERS
            lam_init = 0.8 - 0.6 * math.exp(-0.3 * l)
            h = h + diff_layer(h, b_norm_g[j], b_in_w[j], b_lambda_q1[j], b_lambda_k1[j],
                               b_lambda_q2[j], b_lambda_k2[j], b_subln_g[j], b_out_w[j],
                               k1, k2, v, cos, sin, lam_init)
    return rmsnorm(h, final_norm_g)
```

```python
import functools
import math

import jax
import jax.numpy as jnp
from jax import lax
from jax.experimental import pallas as pl
from jax.experimental.pallas import tpu as pltpu

F32 = jnp.float32
BF16 = jnp.bfloat16

D_MODEL = 1024
SSM_GROUPS = 64
SSM_GROUP = 16
SSM_STATE = 64
N_HEADS = 8
HEAD_DIM = 64
V_HEAD_DIM = 128
ROT_DIM = 16
ROPE_THETA = 500000.0
NORM_EPS = 1e-6
SUBLN_EPS = 1e-5
N_A_LAYERS = 1

LANES = 128
SSM_L = 8
GROUPS_PER_TILE = LANES // SSM_GROUP
N_TILES = D_MODEL // LANES
TILE_STATE = GROUPS_PER_TILE * SSM_STATE
MASK_VALUE = -1e30
VMEM_LIMIT = 56 * 1024 * 1024


def _cparams(sem):
    return pltpu.CompilerParams(dimension_semantics=sem, vmem_limit_bytes=VMEM_LIMIT)


def _inproj_kernel(x_ref, g_ref, w_ref, u_ref, z_ref):
    x = x_ref[0]
    xn = x * lax.rsqrt(jnp.mean(x * x, axis=-1, keepdims=True) + NORM_EPS) * g_ref[...]
    uz = jnp.dot(xn.astype(BF16), w_ref[...], preferred_element_type=F32)
    for kt in range(N_TILES):
        u_ref[0, kt, 0] = uz[:, kt * LANES:(kt + 1) * LANES].astype(BF16)
    z_ref[0, 0] = uz[:, D_MODEL:].astype(BF16)


def _inproj(x3, g, w, rows):
    bsz, nchunk, _ = x3.shape
    grid = (bsz, nchunk // rows, SSM_L)
    return pl.pallas_call(
        _inproj_kernel,
        grid=grid,
        in_specs=[
            pl.BlockSpec((1, rows, D_MODEL), lambda b, c, j: (b, c, j)),
            pl.BlockSpec((1, D_MODEL), lambda b, c, j: (0, 0)),
            pl.BlockSpec((D_MODEL, 2 * D_MODEL), lambda b, c, j: (0, 0)),
        ],
        out_specs=[
            pl.BlockSpec((1, N_TILES, 1, rows, LANES), lambda b, c, j: (b, 0, j, c, 0)),
            pl.BlockSpec((1, 1, rows, D_MODEL), lambda b, c, j: (b, j, c, 0)),
        ],
        out_shape=[
            jax.ShapeDtypeStruct((bsz, N_TILES, SSM_L, nchunk, LANES), BF16),
            jax.ShapeDtypeStruct((bsz, SSM_L, nchunk, D_MODEL), BF16),
        ],
        compiler_params=_cparams(("parallel", "parallel", "arbitrary")),
        name="s5_inproj",
    )(x3, g, w)


def _ssm_tables(lam_re, lam_im, log_dt, b_re, b_im, c_re, c_im, d_skip):
    hi = lax.Precision.HIGHEST
    n_l = SSM_L
    dt = jnp.exp(log_dt)[:, None]
    lr = lam_re * dt
    li = lam_im * dt
    mag = jnp.exp(lr)
    abar_re = mag * jnp.cos(li)
    abar_im = mag * jnp.sin(li)
    den = lam_re * lam_re + lam_im * lam_im
    nr = abar_re - 1.0
    ni = abar_im
    f_re = (nr * lam_re + ni * lam_im) / den
    f_im = (ni * lam_re - nr * lam_im) / den
    bbar_re = f_re[..., None] * b_re - f_im[..., None] * b_im
    bbar_im = f_re[..., None] * b_im + f_im[..., None] * b_re
    n = jnp.arange(n_l + 1, dtype=F32)[:, None, None]
    pmag = jnp.exp(n * lr[None])
    pw_re = pmag * jnp.cos(n * li[None])
    pw_im = pmag * jnp.sin(n * li[None])
    ab_re = pw_re[..., None] * bbar_re[None] - pw_im[..., None] * bbar_im[None]
    ab_im = pw_re[..., None] * bbar_im[None] + pw_im[..., None] * bbar_re[None]
    kd = (jnp.einsum('gop,tgpi->tgio', c_re, ab_re[:n_l], precision=hi)
          - jnp.einsum('gop,tgpi->tgio', c_im, ab_im[:n_l], precision=hi))
    d_diag = d_skip.reshape(SSM_GROUPS, SSM_GROUP)[:, :, None] * jnp.eye(SSM_GROUP, dtype=F32)
    kd = kd.at[0].add(d_diag)
    ca_re = c_re[None] * pw_re[1:, :, None, :] - c_im[None] * pw_im[1:, :, None, :]
    ca_im = c_re[None] * pw_im[1:, :, None, :] + c_im[None] * pw_re[1:, :, None, :]

    eye = jnp.eye(GROUPS_PER_TILE, dtype=F32)
    gt = GROUPS_PER_TILE

    def bd(a):
        l, _, r, c = a.shape
        a = a.reshape(l, N_TILES, gt, r, c)
        a = a[:, :, :, :, None, :] * eye[None, None, :, None, :, None]
        return a.reshape(l, N_TILES, gt * r, gt * c)

    kd_bd = bd(kd)
    in_re = bd(jnp.flip(ab_re[:n_l], 0).transpose(0, 1, 3, 2))
    in_im = bd(jnp.flip(ab_im[:n_l], 0).transpose(0, 1, 3, 2))
    w_in = jnp.concatenate([in_re, in_im], axis=-1)
    w_in = w_in.transpose(1, 0, 2, 3).reshape(N_TILES, n_l * LANES, 2 * TILE_STATE)
    out_re = bd(ca_re.transpose(0, 1, 3, 2))
    out_im = bd(-ca_im.transpose(0, 1, 3, 2))
    w_st = jnp.concatenate([out_re, out_im], axis=2)
    rows = []
    for t in range(n_l):
        blocks = [w_st[t]]
        for j in range(n_l):
            blocks.append(kd_bd[t - j] if j <= t else jnp.zeros_like(kd_bd[0]))
        rows.append(jnp.concatenate(blocks, axis=1))
    w_out = jnp.stack(rows, axis=1)
    w_out = w_out.reshape(N_TILES, n_l // 2, 2, w_out.shape[2], LANES)
    w_out = w_out.transpose(0, 1, 3, 2, 4).reshape(N_TILES, n_l // 2, -1, 2 * LANES)
    a_l = jnp.concatenate([pw_re[n_l].reshape(N_TILES, 1, TILE_STATE),
                           pw_im[n_l].reshape(N_TILES, 1, TILE_STATE)], axis=-1)
    return w_in.astype(BF16), w_out.astype(BF16), a_l


def _ssm_kernel(u_ref, win_ref, wout_ref, al_ref, y_ref, su_scr, w_scr, carry_scr):
    rows = w_scr.shape[0]
    ts = TILE_STATE

    @pl.when(pl.program_id(2) == 0)
    def _():
        carry_scr[...] = jnp.zeros_like(carry_scr)

    for j in range(SSM_L):
        su_scr[:, 2 * ts + j * LANES:2 * ts + (j + 1) * LANES] = u_ref[0, 0, j]
    w_scr[...] = jnp.dot(su_scr[:, 2 * ts:], win_ref[0], preferred_element_type=F32)

    a_re = al_ref[0, :, :ts]
    a_im = al_ref[0, :, ts:]

    def body(c, carry):
        s_re, s_im = carry
        w_re = w_scr[pl.ds(c, 1), :ts]
        w_im = w_scr[pl.ds(c, 1), ts:]
        w_scr[pl.ds(c, 1), :ts] = s_re
        w_scr[pl.ds(c, 1), ts:] = s_im
        return (a_re * s_re - a_im * s_im + w_re, a_re * s_im + a_im * s_re + w_im)

    s_re, s_im = lax.fori_loop(0, rows, body, (carry_scr[0:1, :ts], carry_scr[0:1, ts:]))
    carry_scr[0:1, :ts] = s_re
    carry_scr[0:1, ts:] = s_im

    su_scr[:, :2 * ts] = w_scr[...].astype(BF16)
    for q in range(SSM_L // 2):
        k_used = 2 * ts + (2 * q + 2) * LANES
        yy = jnp.dot(su_scr[:, :k_used], wout_ref[0, q, :k_used, :], preferred_element_type=F32)
        y_ref[0, 0, 2 * q] = yy[:, :LANES].astype(BF16)
        y_ref[0, 0, 2 * q + 1] = yy[:, LANES:].astype(BF16)


def _ssm(u, w_in, w_out, a_l, rows):
    bsz, _, _, nchunk, _ = u.shape
    k_all = 2 * TILE_STATE + SSM_L * LANES
    grid = (N_TILES, bsz, nchunk // rows)
    return pl.pallas_call(
        _ssm_kernel,
        grid=grid,
        in_specs=[
            pl.BlockSpec((1, 1, SSM_L, rows, LANES), lambda t, b, i: (b, t, 0, i, 0)),
            pl.BlockSpec((1, SSM_L * LANES, 2 * TILE_STATE), lambda t, b, i: (t, 0, 0)),
            pl.BlockSpec((1, SSM_L // 2, k_all, 2 * LANES), lambda t, b, i: (t, 0, 0, 0)),
            pl.BlockSpec((1, 1, 2 * TILE_STATE), lambda t, b, i: (t, 0, 0)),
        ],
        out_specs=pl.BlockSpec((1, 1, SSM_L, rows, LANES), lambda t, b, i: (b, t, 0, i, 0)),
        out_shape=jax.ShapeDtypeStruct(u.shape, BF16),
        scratch_shapes=[
            pltpu.VMEM((rows, k_all), BF16),
            pltpu.VMEM((rows, 2 * TILE_STATE), F32),
            pltpu.VMEM((8, 2 * TILE_STATE), F32),
        ],
        compiler_params=_cparams(("parallel", "parallel", "arbitrary")),
        name="s5_scan",
    )(u, w_in, w_out, a_l)


def _rope(t, cos_f, sin_lo, sin_hi):
    half = ROT_DIM // 2
    width = t.shape[1]
    return (t * cos_f + pltpu.roll(t, half, 1) * sin_hi
            + pltpu.roll(t, width - half, 1) * sin_lo)


def _mid_kernel(y_ref, z_ref, x_ref, cos_ref, slo_ref, shi_ref, gluw_ref, glub_ref, outw_ref,
                kvg_ref, kvw_ref, bg_ref, binw_ref,
                h_ref, q_ref, k_ref, v_ref, z2_ref):
    y = jnp.concatenate([y_ref[0, kt, 0] for kt in range(N_TILES)], axis=1).astype(F32)
    y = 0.5 * y * (1.0 + lax.erf(y * (2.0 ** -0.5)))
    gate = jnp.dot(y.astype(BF16), gluw_ref[...], preferred_element_type=F32) + glub_ref[...]
    y = y * jax.nn.sigmoid(gate)
    y = y * jax.nn.silu(z_ref[0, 0].astype(F32))
    h = x_ref[0] + jnp.dot(y.astype(BF16), outw_ref[...], preferred_element_type=F32)
    h_ref[0] = h

    hn = h * lax.rsqrt(jnp.mean(h * h, axis=-1, keepdims=True) + NORM_EPS)
    reps = D_MODEL // LANES
    cos_f = jnp.concatenate([cos_ref[...]] * reps, axis=1)
    sin_lo = jnp.concatenate([slo_ref[...]] * reps, axis=1)
    sin_hi = jnp.concatenate([shi_ref[...]] * reps, axis=1)

    kv = jnp.dot((hn * kvg_ref[...]).astype(BF16), kvw_ref[...], preferred_element_type=F32)
    k_ref[0] = _rope(kv[:, :D_MODEL], cos_f, sin_lo, sin_hi).astype(BF16)
    v_ref[0] = kv[:, D_MODEL:].astype(BF16)

    qz = jnp.dot((hn * bg_ref[...]).astype(BF16), binw_ref[...], preferred_element_type=F32)
    q = _rope(qz[:, :D_MODEL], cos_f, sin_lo, sin_hi) * (HEAD_DIM ** -0.5)
    q_ref[0] = q.astype(BF16)
    z2_ref[0] = qz[:, D_MODEL:].astype(BF16)


def _mid(y, z, x3, cos_t, slo_t, shi_t, glu_w, glu_b, out_w, kv_g, kv_w, b_g, bin_w, rows):
    bsz, nchunk, _ = x3.shape
    grid = (bsz, nchunk // rows, SSM_L)
    tok = pl.BlockSpec((1, rows, D_MODEL), lambda b, c, j: (b, c, j))
    tab = pl.BlockSpec((rows, LANES), lambda b, c, j: (c, j))

    def full(shape):
        return pl.BlockSpec(shape, lambda b, c, j: (0,) * len(shape))

    wide = jax.ShapeDtypeStruct((bsz, nchunk, SSM_L * D_MODEL), BF16)
    return pl.pallas_call(
        _mid_kernel,
        grid=grid,
        in_specs=[
            pl.BlockSpec((1, N_TILES, 1, rows, LANES), lambda b, c, j: (b, 0, j, c, 0)),
            pl.BlockSpec((1, 1, rows, D_MODEL), lambda b, c, j: (b, j, c, 0)),
            tok, tab, tab, tab,
            full((D_MODEL, D_MODEL)), full((1, D_MODEL)), full((D_MODEL, D_MODEL)),
            full((1, D_MODEL)), full((D_MODEL, 2 * D_MODEL)),
            full((1, D_MODEL)), full((D_MODEL, 2 * D_MODEL)),
        ],
        out_specs=[tok, tok, tok, tok, tok],
        out_shape=[jax.ShapeDtypeStruct((bsz, nchunk, SSM_L * D_MODEL), F32), wide, wide, wide, wide],
        compiler_params=_cparams(("parallel", "parallel", "arbitrary")),
        name="s5_out_qkv",
    )(y, z, x3, cos_t, slo_t, shi_t, glu_w, glu_b, out_w, kv_g, kv_w, b_g, bin_w)


def _attn_kernel(q_ref, k_ref, v_ref, lam_ref, g_ref, o_ref, qs_scr, m_scr, l_scr, acc_scr,
                 *, lam_init):
    tq = q_ref.shape[1]
    qi = pl.program_id(2)
    q = q_ref[0]
    lane = lax.broadcasted_iota(jnp.int32, q.shape, 1)
    zero = jnp.zeros_like(q)
    qs_scr[:tq] = jnp.where(lane < HEAD_DIM, q, zero)
    qs_scr[tq:] = jnp.where(lane >= HEAD_DIM, q, zero)
    m_scr[...] = jnp.full_like(m_scr, MASK_VALUE)
    l_scr[...] = jnp.zeros_like(l_scr)
    acc_scr[...] = jnp.zeros_like(acc_scr)

    def step(kb, masked):
        start = pl.multiple_of(kb * tq, tq)
        k = k_ref[0, pl.ds(start, tq), :]
        v = v_ref[0, pl.ds(start, tq), :]
        s = lax.dot_general(qs_scr[...], k, (((1,), (1,)), ((), ())),
                            preferred_element_type=F32)
        if masked:
            row = lax.broadcasted_iota(jnp.int32, s.shape, 0)
            row = jnp.where(row >= tq, row - tq, row)
            col = lax.broadcasted_iota(jnp.int32, s.shape, 1)
            s = jnp.where(col <= row, s, MASK_VALUE)
        m_prev = m_scr[...]
        m_new = jnp.maximum(m_prev, jnp.max(s, axis=-1, keepdims=True))
        alpha = jnp.exp(m_prev - m_new)
        p = jnp.exp(s - m_new)
        l_scr[...] = alpha * l_scr[...] + jnp.sum(p, axis=-1, keepdims=True)
        acc_scr[...] = alpha * acc_scr[...] + jnp.dot(p.astype(BF16), v,
                                                      preferred_element_type=F32)
        m_scr[...] = m_new

    def body(kb, carry):
        step(kb, False)
        return carry

    lax.fori_loop(0, qi, body, 0)
    step(qi, True)

    o_all = acc_scr[...] / l_scr[...]
    o = o_all[:tq] - lam_ref[...] * o_all[tq:]
    o = o * lax.rsqrt(jnp.mean(o * o, axis=-1, keepdims=True) + SUBLN_EPS)
    o_ref[0] = ((o * g_ref[...]) * (1.0 - lam_init)).astype(BF16)


def _attn(q, k, v, lam_row, subln_g, lam_init, tq):
    bsz, seq, _ = q.shape
    grid = (bsz, N_HEADS, seq // tq)
    return pl.pallas_call(
        functools.partial(_attn_kernel, lam_init=lam_init),
        grid=grid,
        in_specs=[
            pl.BlockSpec((1, tq, V_HEAD_DIM), lambda b, h, i: (b, i, h)),
            pl.BlockSpec((1, seq, V_HEAD_DIM), lambda b, h, i: (b, 0, h)),
            pl.BlockSpec((1, seq, V_HEAD_DIM), lambda b, h, i: (b, 0, h)),
            pl.BlockSpec((1, V_HEAD_DIM), lambda b, h, i: (0, 0)),
            pl.BlockSpec((1, V_HEAD_DIM), lambda b, h, i: (0, 0)),
        ],
        out_specs=pl.BlockSpec((1, tq, V_HEAD_DIM), lambda b, h, i: (b, i, h)),
        out_shape=jax.ShapeDtypeStruct((bsz, seq, N_HEADS * V_HEAD_DIM), BF16),
        scratch_shapes=[
            pltpu.VMEM((2 * tq, V_HEAD_DIM), BF16),
            pltpu.VMEM((2 * tq, 1), F32),
            pltpu.VMEM((2 * tq, 1), F32),
            pltpu.VMEM((2 * tq, V_HEAD_DIM), F32),
        ],
        compiler_params=_cparams(("parallel", "parallel", "arbitrary")),
        name="diff_attn",
    )(q, k, v, lam_row, subln_g)


def _final_kernel(o_ref, z_ref, h_ref, w_ref, g_ref, out_ref):
    y = o_ref[0].astype(F32) * jax.nn.silu(z_ref[0].astype(F32))
    h = h_ref[0] + jnp.dot(y.astype(BF16), w_ref[...], preferred_element_type=F32)
    out_ref[0] = h * lax.rsqrt(jnp.mean(h * h, axis=-1, keepdims=True) + NORM_EPS) * g_ref[...]


def _final(o, z2, h1, out_w, g, rows):
    bsz, seq, _ = h1.shape
    tok = pl.BlockSpec((1, rows, D_MODEL), lambda b, i: (b, i, 0))
    return pl.pallas_call(
        _final_kernel,
        grid=(bsz, seq // rows),
        in_specs=[tok, tok, tok,
                  pl.BlockSpec((D_MODEL, D_MODEL), lambda b, i: (0, 0)),
                  pl.BlockSpec((1, D_MODEL), lambda b, i: (0, 0))],
        out_specs=tok,
        out_shape=jax.ShapeDtypeStruct(h1.shape, F32),
        compiler_params=_cparams(("parallel", "parallel")),
        name="attn_out_final",
    )(o, z2, h1, out_w, g)


def _rope_tables(seq):
    half = ROT_DIM // 2
    inv = ROPE_THETA ** (-jnp.arange(0, ROT_DIM, 2, dtype=F32) / ROT_DIM)
    ang = jnp.arange(seq, dtype=F32)[:, None] * inv[None, :]
    cos, sin = jnp.cos(ang), jnp.sin(ang)
    ones = jnp.ones((seq, HEAD_DIM - ROT_DIM), F32)
    zeros = jnp.zeros((seq, HEAD_DIM - ROT_DIM), F32)
    zh = jnp.zeros((seq, half), F32)
    cos_h = jnp.concatenate([cos, cos, ones], axis=1)
    lo_h = jnp.concatenate([-sin, zh, zeros], axis=1)
    hi_h = jnp.concatenate([zh, sin, zeros], axis=1)
    rep = LANES // HEAD_DIM
    return jnp.tile(cos_h, (1, rep)), jnp.tile(lo_h, (1, rep)), jnp.tile(hi_h, (1, rep))


def _pick(n, want):
    r = min(n, want)
    assert n % r == 0, (n, want)
    return r


def kernel(x, a_norm_g, a_in_w, a_lambda_re, a_lambda_im, a_log_dt, a_b_re, a_b_im, a_c_re, a_c_im, a_d, a_glu_w, a_glu_b, a_out_w, kv_norm_g, kv_w, b_norm_g, b_in_w, b_lambda_q1, b_lambda_k1, b_lambda_q2, b_lambda_k2, b_subln_g, b_out_w, final_norm_g):
    bsz, seq, d = x.shape
    assert d == D_MODEL and seq % SSM_L == 0
    nchunk = seq // SSM_L
    x3 = x.reshape(bsz, nchunk, SSM_L * D_MODEL)

    rows_tok = _pick(nchunk, 256)
    u, z = _inproj(x3, a_norm_g[0][None], a_in_w[0].astype(BF16), rows_tok)
    w_in, w_out, a_l = _ssm_tables(a_lambda_re[0], a_lambda_im[0], a_log_dt[0], a_b_re[0],
                                   a_b_im[0], a_c_re[0], a_c_im[0], a_d[0])
    y = _ssm(u, w_in, w_out, a_l, _pick(nchunk, 512))

    cos_t, slo_t, shi_t = (t.reshape(nchunk, SSM_L * LANES) for t in _rope_tables(seq))
    h1, q, k, v, z2 = _mid(y, z, x3, cos_t, slo_t, shi_t,
                           a_glu_w[0].astype(BF16), a_glu_b[0][None], a_out_w[0].astype(BF16),
                           kv_norm_g[None], kv_w.astype(BF16),
                           b_norm_g[0][None], b_in_w[0].astype(BF16), rows_tok)
    h1 = h1.reshape(bsz, seq, D_MODEL)
    q, k, v, z2 = (t.reshape(bsz, seq, D_MODEL) for t in (q, k, v, z2))

    lam_init = 0.8 - 0.6 * math.exp(-0.3 * N_A_LAYERS)
    lam = (jnp.exp(jnp.sum(b_lambda_q1[0] * b_lambda_k1[0]))
           - jnp.exp(jnp.sum(b_lambda_q2[0] * b_lambda_k2[0])) + lam_init)
    lam_row = jnp.full((1, V_HEAD_DIM), lam, F32)
    o = _attn(q, k, v, lam_row, b_subln_g[0][None], lam_init, _pick(seq, 512))

    return _final(o, z2, h1, b_out_w[0].astype(BF16), final_norm_g[None], _pick(seq, 512))
```

```python
import functools
import math

import jax
import jax.numpy as jnp
from jax import lax
from jax.experimental import pallas as pl
from jax.experimental.pallas import tpu as pltpu

F32 = jnp.float32
BF16 = jnp.bfloat16

D_MODEL = 1024
SSM_GROUPS = 64
SSM_GROUP = 16
SSM_STATE = 64
N_HEADS = 8
HEAD_DIM = 64
V_HEAD_DIM = 128
ROT_DIM = 16
ROPE_THETA = 500000.0
NORM_EPS = 1e-6
SUBLN_EPS = 1e-5
N_A_LAYERS = 1

LANES = 128
SSM_L = 8
GROUPS_PER_TILE = LANES // SSM_GROUP
N_TILES = D_MODEL // LANES
TILE_STATE = GROUPS_PER_TILE * SSM_STATE
MASK_VALUE = -1e30
VMEM_LIMIT = 56 * 1024 * 1024


def _cparams(sem):
    return pltpu.CompilerParams(dimension_semantics=sem, vmem_limit_bytes=VMEM_LIMIT)


def _inproj_kernel(x_ref, g_ref, w_ref, u_ref, z_ref):
    x = x_ref[0]
    xn = x * lax.rsqrt(jnp.mean(x * x, axis=-1, keepdims=True) + NORM_EPS) * g_ref[...]
    uz = jnp.dot(xn.astype(BF16), w_ref[...], preferred_element_type=F32)
    for kt in range(N_TILES):
        u_ref[0, kt, 0] = uz[:, kt * LANES:(kt + 1) * LANES].astype(BF16)
    z_ref[0, 0] = uz[:, D_MODEL:].astype(BF16)


def _inproj(x3, g, w, rows):
    bsz, nchunk, _ = x3.shape
    grid = (bsz, nchunk // rows, SSM_L)
    return pl.pallas_call(
        _inproj_kernel,
        grid=grid,
        in_specs=[
            pl.BlockSpec((1, rows, D_MODEL), lambda b, c, j: (b, c, j)),
            pl.BlockSpec((1, D_MODEL), lambda b, c, j: (0, 0)),
            pl.BlockSpec((D_MODEL, 2 * D_MODEL), lambda b, c, j: (0, 0)),
        ],
        out_specs=[
            pl.BlockSpec((1, N_TILES, 1, rows, LANES), lambda b, c, j: (b, 0, j, c, 0)),
            pl.BlockSpec((1, 1, rows, D_MODEL), lambda b, c, j: (b, j, c, 0)),
        ],
        out_shape=[
            jax.ShapeDtypeStruct((bsz, N_TILES, SSM_L, nchunk, LANES), BF16),
            jax.ShapeDtypeStruct((bsz, SSM_L, nchunk, D_MODEL), BF16),
        ],
        compiler_params=_cparams(("parallel", "parallel", "arbitrary")),
        name="s5_inproj",
    )(x3, g, w)


def _ssm_tables(lam_re, lam_im, log_dt, b_re, b_im, c_re, c_im, d_skip):
    hi = lax.Precision.HIGHEST
    n_l = SSM_L
    dt = jnp.exp(log_dt)[:, None]
    lr = lam_re * dt
    li = lam_im * dt
    mag = jnp.exp(lr)
    abar_re = mag * jnp.cos(li)
    abar_im = mag * jnp.sin(li)
    den = lam_re * lam_re + lam_im * lam_im
    nr = abar_re - 1.0
    ni = abar_im
    f_re = (nr * lam_re + ni * lam_im) / den
    f_im = (ni * lam_re - nr * lam_im) / den
    bbar_re = f_re[..., None] * b_re - f_im[..., None] * b_im
    bbar_im = f_re[..., None] * b_im + f_im[..., None] * b_re
    n = jnp.arange(n_l + 1, dtype=F32)[:, None, None]
    pmag = jnp.exp(n * lr[None])
    pw_re = pmag * jnp.cos(n * li[None])
    pw_im = pmag * jnp.sin(n * li[None])
    ab_re = pw_re[..., None] * bbar_re[None] - pw_im[..., None] * bbar_im[None]
    ab_im = pw_re[..., None] * bbar_im[None] + pw_im[..., None] * bbar_re[None]
    kd = (jnp.einsum('gop,tgpi->tgio', c_re, ab_re[:n_l], precision=hi)
          - jnp.einsum('gop,tgpi->tgio', c_im, ab_im[:n_l], precision=hi))
    d_diag = d_skip.reshape(SSM_GROUPS, SSM_GROUP)[:, :, None] * jnp.eye(SSM_GROUP, dtype=F32)
    kd = kd.at[0].add(d_diag)
    ca_re = c_re[None] * pw_re[1:, :, None, :] - c_im[None] * pw_im[1:, :, None, :]
    ca_im = c_re[None] * pw_im[1:, :, None, :] + c_im[None] * pw_re[1:, :, None, :]

    eye = jnp.eye(GROUPS_PER_TILE, dtype=F32)
    gt = GROUPS_PER_TILE

    def bd(a):
        l, _, r, c = a.shape
        a = a.reshape(l, N_TILES, gt, r, c)
        a = a[:, :, :, :, None, :] * eye[None, None, :, None, :, None]
        return a.reshape(l, N_TILES, gt * r, gt * c)

    kd_bd = bd(kd)
    in_re = bd(jnp.flip(ab_re[:n_l], 0).transpose(0, 1, 3, 2))
    in_im = bd(jnp.flip(ab_im[:n_l], 0).transpose(0, 1, 3, 2))
    w_in = jnp.concatenate([in_re, in_im], axis=-1)
    w_in = w_in.transpose(1, 0, 2, 3).reshape(N_TILES, n_l * LANES, 2 * TILE_STATE)
    out_re = bd(ca_re.transpose(0, 1, 3, 2))
    out_im = bd(-ca_im.transpose(0, 1, 3, 2))
    w_st = jnp.concatenate([out_re, out_im], axis=2)
    rows = []
    for t in range(n_l):
        blocks = [w_st[t]]
        for j in range(n_l):
            blocks.append(kd_bd[t - j] if j <= t else jnp.zeros_like(kd_bd[0]))
        rows.append(jnp.concatenate(blocks, axis=1))
    w_out = jnp.stack(rows, axis=1)
    w_out = w_out.reshape(N_TILES, n_l // 2, 2, w_out.shape[2], LANES)
    w_out = w_out.transpose(0, 1, 3, 2, 4).reshape(N_TILES, n_l // 2, -1, 2 * LANES)
    a_l = jnp.concatenate([pw_re[n_l].reshape(N_TILES, 1, TILE_STATE),
                           pw_im[n_l].reshape(N_TILES, 1, TILE_STATE)], axis=-1)
    return w_in.astype(BF16), w_out.astype(BF16), a_l


def _ssm_kernel(u_ref, win_ref, wout_ref, al_ref, y_ref, su_scr, w_scr, carry_scr):
    rows = w_scr.shape[0]
    ts = TILE_STATE

    @pl.when(pl.program_id(2) == 0)
    def _():
        carry_scr[...] = jnp.zeros_like(carry_scr)

    for j in range(SSM_L):
        su_scr[:, 2 * ts + j * LANES:2 * ts + (j + 1) * LANES] = u_ref[0, 0, j]
    w_scr[...] = jnp.dot(su_scr[:, 2 * ts:], win_ref[0], preferred_element_type=F32)

    a_re = al_ref[0, :, :ts]
    a_im = al_ref[0, :, ts:]

    def body(c, carry):
        s_re, s_im = carry
        w_re = w_scr[pl.ds(c, 1), :ts]
        w_im = w_scr[pl.ds(c, 1), ts:]
        w_scr[pl.ds(c, 1), :ts] = s_re
        w_scr[pl.ds(c, 1), ts:] = s_im
        return (a_re * s_re - a_im * s_im + w_re, a_re * s_im + a_im * s_re + w_im)

    s_re, s_im = lax.fori_loop(0, rows, body, (carry_scr[0:1, :ts], carry_scr[0:1, ts:]))
    carry_scr[0:1, :ts] = s_re
    carry_scr[0:1, ts:] = s_im

    su_scr[:, :2 * ts] = w_scr[...].astype(BF16)
    for q in range(SSM_L // 2):
        k_used = 2 * ts + (2 * q + 2) * LANES
        yy = jnp.dot(su_scr[:, :k_used], wout_ref[0, q, :k_used, :], preferred_element_type=F32)
        y_ref[0, 0, 2 * q] = yy[:, :LANES].astype(BF16)
        y_ref[0, 0, 2 * q + 1] = yy[:, LANES:].astype(BF16)


def _ssm(u, w_in, w_out, a_l, rows):
    bsz, _, _, nchunk, _ = u.shape
    k_all = 2 * TILE_STATE + SSM_L * LANES
    grid = (N_TILES, bsz, nchunk // rows)
    return pl.pallas_call(
        _ssm_kernel,
        grid=grid,
        in_specs=[
            pl.BlockSpec((1, 1, SSM_L, rows, LANES), lambda t, b, i: (b, t, 0, i, 0)),
            pl.BlockSpec((1, SSM_L * LANES, 2 * TILE_STATE), lambda t, b, i: (t, 0, 0)),
            pl.BlockSpec((1, SSM_L // 2, k_all, 2 * LANES), lambda t, b, i: (t, 0, 0, 0)),
            pl.BlockSpec((1, 1, 2 * TILE_STATE), lambda t, b, i: (t, 0, 0)),
        ],
        out_specs=pl.BlockSpec((1, 1, SSM_L, rows, LANES), lambda t, b, i: (b, t, 0, i, 0)),
        out_shape=jax.ShapeDtypeStruct(u.shape, BF16),
        scratch_shapes=[
            pltpu.VMEM((rows, k_all), BF16),
            pltpu.VMEM((rows, 2 * TILE_STATE), F32),
            pltpu.VMEM((8, 2 * TILE_STATE), F32),
        ],
        compiler_params=_cparams(("parallel", "parallel", "arbitrary")),
        name="s5_scan",
    )(u, w_in, w_out, a_l)


def _rope(t, cos_f, sin_lo, sin_hi):
    half = ROT_DIM // 2
    width = t.shape[1]
    return (t * cos_f + pltpu.roll(t, half, 1) * sin_hi
            + pltpu.roll(t, width - half, 1) * sin_lo)


def _mid_kernel(y_ref, z_ref, x_ref, cos_ref, slo_ref, shi_ref, gluw_ref, glub_ref, outw_ref,
                kvg_ref, kvw_ref, bg_ref, binw_ref,
                h_ref, q_ref, k_ref, v_ref, z2_ref):
    y = jnp.concatenate([y_ref[0, kt, 0] for kt in range(N_TILES)], axis=1).astype(F32)
    y = 0.5 * y * (1.0 + lax.erf(y * (2.0 ** -0.5)))
    gate = jnp.dot(y.astype(BF16), gluw_ref[...], preferred_element_type=F32) + glub_ref[...]
    y = y * jax.nn.sigmoid(gate)
    y = y * jax.nn.silu(z_ref[0, 0].astype(F32))
    h = x_ref[0] + jnp.dot(y.astype(BF16), outw_ref[...], preferred_element_type=F32)
    h_ref[0] = h

    hn = h * lax.rsqrt(jnp.mean(h * h, axis=-1, keepdims=True) + NORM_EPS)
    reps = D_MODEL // LANES
    cos_f = jnp.concatenate([cos_ref[...]] * reps, axis=1)
    sin_lo = jnp.concatenate([slo_ref[...]] * reps, axis=1)
    sin_hi = jnp.concatenate([shi_ref[...]] * reps, axis=1)

    kv = jnp.dot((hn * kvg_ref[...]).astype(BF16), kvw_ref[...], preferred_element_type=F32)
    k_ref[0] = _rope(kv[:, :D_MODEL], cos_f, sin_lo, sin_hi).astype(BF16)
    v_ref[0] = kv[:, D_MODEL:].astype(BF16)

    qz = jnp.dot((hn * bg_ref[...]).astype(BF16), binw_ref[...], preferred_element_type=F32)
    q = _rope(qz[:, :D_MODEL], cos_f, sin_lo, sin_hi) * (HEAD_DIM ** -0.5)
    q_ref[0] = q.astype(BF16)
    z2_ref[0] = qz[:, D_MODEL:].astype(BF16)


def _mid(y, z, x3, cos_t, slo_t, shi_t, glu_w, glu_b, out_w, kv_g, kv_w, b_g, bin_w, rows):
    bsz, nchunk, _ = x3.shape
    grid = (bsz, nchunk // rows, SSM_L)
    tok = pl.BlockSpec((1, rows, D_MODEL), lambda b, c, j: (b, c, j))
    tab = pl.BlockSpec((rows, LANES), lambda b, c, j: (c, j))

    def full(shape):
        return pl.BlockSpec(shape, lambda b, c, j: (0,) * len(shape))

    wide = jax.ShapeDtypeStruct((bsz, nchunk, SSM_L * D_MODEL), BF16)
    return pl.pallas_call(
        _mid_kernel,
        grid=grid,
        in_specs=[
            pl.BlockSpec((1, N_TILES, 1, rows, LANES), lambda b, c, j: (b, 0, j, c, 0)),
            pl.BlockSpec((1, 1, rows, D_MODEL), lambda b, c, j: (b, j, c, 0)),
            tok, tab, tab, tab,
            full((D_MODEL, D_MODEL)), full((1, D_MODEL)), full((D_MODEL, D_MODEL)),
            full((1, D_MODEL)), full((D_MODEL, 2 * D_MODEL)),
            full((1, D_MODEL)), full((D_MODEL, 2 * D_MODEL)),
        ],
        out_specs=[tok, tok, tok, tok, tok],
        out_shape=[jax.ShapeDtypeStruct((bsz, nchunk, SSM_L * D_MODEL), F32), wide, wide, wide, wide],
        compiler_params=_cparams(("parallel", "parallel", "arbitrary")),
        name="s5_out_qkv",
    )(y, z, x3, cos_t, slo_t, shi_t, glu_w, glu_b, out_w, kv_g, kv_w, b_g, bin_w)


V_ROWS = V_HEAD_DIM + 16


def _attn_kernel(q_ref, k_ref, v_ref, lam_ref, g_ref, o_ref,
                 qs_scr, vt_scr, sa_scr, sb_scr, m_scr, acc_scr, *, lam_init):
    tq = q_ref.shape[1]
    nblk = k_ref.shape[1] // tq
    qi = pl.program_id(2)

    @pl.when(qi == 0)
    def _():
        for c in range(nblk):
            vt = v_ref[0, c * tq:(c + 1) * tq, :].astype(F32).T
            vt_scr[c, :V_HEAD_DIM, :] = vt.astype(BF16)
            row = lax.broadcasted_iota(jnp.int32, (V_ROWS - V_HEAD_DIM, tq), 0)
            vt_scr[c, V_HEAD_DIM:, :] = jnp.where(row == 0, 1.0, 0.0).astype(BF16)

    q = q_ref[0]
    lane = lax.broadcasted_iota(jnp.int32, q.shape, 1)
    zero = jnp.zeros_like(q)
    qs_scr[:tq] = jnp.where(lane < HEAD_DIM, q, zero)
    qs_scr[tq:] = jnp.where(lane >= HEAD_DIM, q, zero)
    m_scr[...] = jnp.full_like(m_scr, MASK_VALUE)
    acc_scr[...] = jnp.zeros_like(acc_scr)

    def scores(kb, s_scr):
        start = pl.multiple_of(kb * tq, tq)
        k = k_ref[0, pl.ds(start, tq), :]
        s_scr[...] = lax.dot_general(k, qs_scr[...], (((1,), (1,)), ((), ())),
                                     preferred_element_type=F32)

    def update(kb, s_scr, masked):
        s = s_scr[...]
        if masked:
            key = lax.broadcasted_iota(jnp.int32, s.shape, 0)
            qry = lax.broadcasted_iota(jnp.int32, s.shape, 1)
            qry = jnp.where(qry >= tq, qry - tq, qry)
            s = jnp.where(key <= qry, s, MASK_VALUE)
        m_prev = m_scr[0:1, :]
        m_new = jnp.maximum(m_prev, jnp.max(s, axis=0, keepdims=True))
        alpha = jnp.exp(m_prev - m_new)
        p = jnp.exp(s - m_new).astype(BF16)
        pv = jnp.dot(vt_scr[kb], p, preferred_element_type=F32)
        acc_scr[...] = alpha * acc_scr[...] + pv
        m_scr[0:1, :] = m_new

    scores(0, sa_scr)
    npairs = qi // 2

    def pair(i, carry):
        kb = 2 * i
        scores(kb + 1, sb_scr)
        update(kb, sa_scr, False)
        scores(kb + 2, sa_scr)
        update(kb + 1, sb_scr, False)
        return carry

    lax.fori_loop(0, npairs, pair, 0)
    odd = qi - 2 * npairs

    @pl.when(odd == 0)
    def _():
        update(qi, sa_scr, True)

    @pl.when(odd == 1)
    def _():
        scores(qi, sb_scr)
        update(qi - 1, sa_scr, False)
        update(qi, sb_scr, True)

    acc = acc_scr[...]
    o_all = acc[:V_HEAD_DIM] / acc[V_HEAD_DIM:V_HEAD_DIM + 1]
    o = o_all[:, :tq] - lam_ref[:, 0:1] * o_all[:, tq:]
    o = o * lax.rsqrt(jnp.mean(o * o, axis=0, keepdims=True) + SUBLN_EPS)
    o = (o * g_ref[...]) * (1.0 - lam_init)
    o_ref[0] = o.T.astype(BF16)


def _attn(q, k, v, lam_row, subln_g, lam_init, tq):
    bsz, seq, _ = q.shape
    grid = (bsz, N_HEADS, seq // tq)
    return pl.pallas_call(
        functools.partial(_attn_kernel, lam_init=lam_init),
        grid=grid,
        in_specs=[
            pl.BlockSpec((1, tq, V_HEAD_DIM), lambda b, h, i: (b, i, h)),
            pl.BlockSpec((1, seq, V_HEAD_DIM), lambda b, h, i: (b, 0, h)),
            pl.BlockSpec((1, seq, V_HEAD_DIM), lambda b, h, i: (b, 0, h)),
            pl.BlockSpec((1, V_HEAD_DIM), lambda b, h, i: (0, 0)),
            pl.BlockSpec((V_HEAD_DIM, 1), lambda b, h, i: (0, 0)),
        ],
        out_specs=pl.BlockSpec((1, tq, V_HEAD_DIM), lambda b, h, i: (b, i, h)),
        out_shape=jax.ShapeDtypeStruct((bsz, seq, N_HEADS * V_HEAD_DIM), BF16),
        scratch_shapes=[
            pltpu.VMEM((2 * tq, V_HEAD_DIM), BF16),
            pltpu.VMEM((seq // tq, V_ROWS, tq), BF16),
            pltpu.VMEM((tq, 2 * tq), F32),
            pltpu.VMEM((tq, 2 * tq), F32),
            pltpu.VMEM((8, 2 * tq), F32),
            pltpu.VMEM((V_ROWS, 2 * tq), F32),
        ],
        compiler_params=_cparams(("parallel", "parallel", "arbitrary")),
        name="diff_attn",
    )(q, k, v, lam_row, subln_g)


def _final_kernel(o_ref, z_ref, h_ref, w_ref, g_ref, out_ref):
    y = o_ref[0].astype(F32) * jax.nn.silu(z_ref[0].astype(F32))
    h = h_ref[0] + jnp.dot(y.astype(BF16), w_ref[...], preferred_element_type=F32)
    out_ref[0] = h * lax.rsqrt(jnp.mean(h * h, axis=-1, keepdims=True) + NORM_EPS) * g_ref[...]


def _final(o, z2, h1, out_w, g, rows):
    bsz, seq, _ = h1.shape
    tok = pl.BlockSpec((1, rows, D_MODEL), lambda b, i: (b, i, 0))
    return pl.pallas_call(
        _final_kernel,
        grid=(bsz, seq // rows),
        in_specs=[tok, tok, tok,
                  pl.BlockSpec((D_MODEL, D_MODEL), lambda b, i: (0, 0)),
                  pl.BlockSpec((1, D_MODEL), lambda b, i: (0, 0))],
        out_specs=tok,
        out_shape=jax.ShapeDtypeStruct(h1.shape, F32),
        compiler_params=_cparams(("parallel", "parallel")),
        name="attn_out_final",
    )(o, z2, h1, out_w, g)


def _rope_tables(seq):
    half = ROT_DIM // 2
    inv = ROPE_THETA ** (-jnp.arange(0, ROT_DIM, 2, dtype=F32) / ROT_DIM)
    ang = jnp.arange(seq, dtype=F32)[:, None] * inv[None, :]
    cos, sin = jnp.cos(ang), jnp.sin(ang)
    ones = jnp.ones((seq, HEAD_DIM - ROT_DIM), F32)
    zeros = jnp.zeros((seq, HEAD_DIM - ROT_DIM), F32)
    zh = jnp.zeros((seq, half), F32)
    cos_h = jnp.concatenate([cos, cos, ones], axis=1)
    lo_h = jnp.concatenate([-sin, zh, zeros], axis=1)
    hi_h = jnp.concatenate([zh, sin, zeros], axis=1)
    rep = LANES // HEAD_DIM
    return jnp.tile(cos_h, (1, rep)), jnp.tile(lo_h, (1, rep)), jnp.tile(hi_h, (1, rep))


def _pick(n, want):
    r = min(n, want)
    assert n % r == 0, (n, want)
    return r


def kernel(x, a_norm_g, a_in_w, a_lambda_re, a_lambda_im, a_log_dt, a_b_re, a_b_im, a_c_re, a_c_im, a_d, a_glu_w, a_glu_b, a_out_w, kv_norm_g, kv_w, b_norm_g, b_in_w, b_lambda_q1, b_lambda_k1, b_lambda_q2, b_lambda_k2, b_subln_g, b_out_w, final_norm_g):
    bsz, seq, d = x.shape
    assert d == D_MODEL and seq % SSM_L == 0
    nchunk = seq // SSM_L
    x3 = x.reshape(bsz, nchunk, SSM_L * D_MODEL)

    rows_tok = _pick(nchunk, 256)
    u, z = _inproj(x3, a_norm_g[0][None], a_in_w[0].astype(BF16), rows_tok)
    w_in, w_out, a_l = _ssm_tables(a_lambda_re[0], a_lambda_im[0], a_log_dt[0], a_b_re[0],
                                   a_b_im[0], a_c_re[0], a_c_im[0], a_d[0])
    y = _ssm(u, w_in, w_out, a_l, _pick(nchunk, 512))

    cos_t, slo_t, shi_t = (t.reshape(nchunk, SSM_L * LANES) for t in _rope_tables(seq))
    h1, q, k, v, z2 = _mid(y, z, x3, cos_t, slo_t, shi_t,
                           a_glu_w[0].astype(BF16), a_glu_b[0][None], a_out_w[0].astype(BF16),
                           kv_norm_g[None], kv_w.astype(BF16),
                           b_norm_g[0][None], b_in_w[0].astype(BF16), rows_tok)
    h1 = h1.reshape(bsz, seq, D_MODEL)
    q, k, v, z2 = (t.reshape(bsz, seq, D_MODEL) for t in (q, k, v, z2))

    lam_init = 0.8 - 0.6 * math.exp(-0.3 * N_A_LAYERS)
    lam = (jnp.exp(jnp.sum(b_lambda_q1[0] * b_lambda_k1[0]))
           - jnp.exp(jnp.sum(b_lambda_q2[0] * b_lambda_k2[0])) + lam_init)
    lam_row = jnp.full((1, V_HEAD_DIM), lam, F32)
    o = _attn(q, k, v, lam_row, b_subln_g[0][:, None], lam_init, _pick(seq, 512))

    return _final(o, z2, h1, b_out_w[0].astype(BF16), final_norm_g[None], _pick(seq, 512))
```

```python
import functools
import math

import jax
import jax.numpy as jnp
from jax import lax
from jax.experimental import pallas as pl
from jax.experimental.pallas import tpu as pltpu

F32 = jnp.float32
BF16 = jnp.bfloat16

D_MODEL = 1024
SSM_GROUPS = 64
SSM_GROUP = 16
SSM_STATE = 64
N_HEADS = 8
HEAD_DIM = 64
V_HEAD_DIM = 128
ROT_DIM = 16
ROPE_THETA = 500000.0
NORM_EPS = 1e-6
SUBLN_EPS = 1e-5
N_A_LAYERS = 1

LANES = 128
SSM_L = 8
GROUPS_PER_TILE = LANES // SSM_GROUP
N_TILES = D_MODEL // LANES
TILE_STATE = GROUPS_PER_TILE * SSM_STATE
BLK = 512
BLK_CHUNKS = BLK // SSM_L
MASK_VALUE = -1e30
VMEM_LIMIT = 56 * 1024 * 1024


def _cparams(sem):
    return pltpu.CompilerParams(dimension_semantics=sem, vmem_limit_bytes=VMEM_LIMIT)


def _const_spec(shape):
    return pl.BlockSpec(shape, lambda *_: (0,) * len(shape), pipeline_mode=pl.Buffered(1))


def _load_phase_major(x_ref):
    return jnp.concatenate([x_ref[0, :, j, :] for j in range(SSM_L)], axis=0)


_NATURAL_TOK = pl.BlockSpec((1, BLK_CHUNKS, SSM_L, D_MODEL), lambda b, c: (b, c, 0, 0))


def _block_position(row):
    return (row % BLK_CHUNKS) * SSM_L + row // BLK_CHUNKS


def _inproj_kernel(x_ref, g_ref, w_ref, u_ref, z_ref):
    x = _load_phase_major(x_ref)
    xn = x * lax.rsqrt(jnp.mean(x * x, axis=-1, keepdims=True) + NORM_EPS) * g_ref[...]
    uz = jnp.dot(xn.astype(BF16), w_ref[...], preferred_element_type=F32)
    for j in range(SSM_L):
        rows = slice(j * BLK_CHUNKS, (j + 1) * BLK_CHUNKS)
        for kt in range(N_TILES):
            u_ref[0, kt, j] = uz[rows, kt * LANES:(kt + 1) * LANES].astype(BF16)
    z_ref[0] = uz[:, D_MODEL:].astype(BF16)


def _inproj(x, g, w):
    bsz, nchunk, _, _ = x.shape
    seq = nchunk * SSM_L
    tok = pl.BlockSpec((1, BLK, D_MODEL), lambda b, c: (b, c, 0))
    return pl.pallas_call(
        _inproj_kernel,
        grid=(bsz, seq // BLK),
        in_specs=[_NATURAL_TOK, _const_spec((1, D_MODEL)), _const_spec((D_MODEL, 2 * D_MODEL))],
        out_specs=[
            pl.BlockSpec((1, N_TILES, SSM_L, BLK_CHUNKS, LANES), lambda b, c: (b, 0, 0, c, 0)),
            tok,
        ],
        out_shape=[
            jax.ShapeDtypeStruct((bsz, N_TILES, SSM_L, nchunk, LANES), BF16),
            jax.ShapeDtypeStruct((bsz, seq, D_MODEL), BF16),
        ],
        compiler_params=_cparams(("parallel", "parallel")),
        name="s5_inproj",
    )(x, g, w)


def _ssm_tables(lam_re, lam_im, log_dt, b_re, b_im, c_re, c_im, d_skip):
    hi = lax.Precision.HIGHEST
    n_l = SSM_L
    dt = jnp.exp(log_dt)[:, None]
    lr = lam_re * dt
    li = lam_im * dt
    mag = jnp.exp(lr)
    abar_re = mag * jnp.cos(li)
    abar_im = mag * jnp.sin(li)
    den = lam_re * lam_re + lam_im * lam_im
    nr = abar_re - 1.0
    ni = abar_im
    f_re = (nr * lam_re + ni * lam_im) / den
    f_im = (ni * lam_re - nr * lam_im) / den
    bbar_re = f_re[..., None] * b_re - f_im[..., None] * b_im
    bbar_im = f_re[..., None] * b_im + f_im[..., None] * b_re
    n = jnp.arange(n_l + 1, dtype=F32)[:, None, None]
    pmag = jnp.exp(n * lr[None])
    pw_re = pmag * jnp.cos(n * li[None])
    pw_im = pmag * jnp.sin(n * li[None])
    ab_re = pw_re[..., None] * bbar_re[None] - pw_im[..., None] * bbar_im[None]
    ab_im = pw_re[..., None] * bbar_im[None] + pw_im[..., None] * bbar_re[None]
    kd = (jnp.einsum('gop,tgpi->tgio', c_re, ab_re[:n_l], precision=hi)
          - jnp.einsum('gop,tgpi->tgio', c_im, ab_im[:n_l], precision=hi))
    d_diag = d_skip.reshape(SSM_GROUPS, SSM_GROUP)[:, :, None] * jnp.eye(SSM_GROUP, dtype=F32)
    kd = kd.at[0].add(d_diag)
    ca_re = c_re[None] * pw_re[1:, :, None, :] - c_im[None] * pw_im[1:, :, None, :]
    ca_im = c_re[None] * pw_im[1:, :, None, :] + c_im[None] * pw_re[1:, :, None, :]

    gt, nt = GROUPS_PER_TILE, N_TILES
    eye = jnp.eye(gt, dtype=F32)

    ab = jnp.flip(jnp.stack([ab_re[:n_l], ab_im[:n_l]], axis=0), axis=1)
    a_c = ab.reshape(2, n_l, nt, gt, SSM_STATE, SSM_GROUP).transpose(2, 1, 3, 5, 0, 4)
    w_in = a_c[:, :, :, :, :, None, :] * eye[None, None, :, None, None, :, None]
    w_in = w_in.reshape(nt, n_l * LANES, 2 * TILE_STATE)

    ca = jnp.stack([ca_re, -ca_im], axis=0)
    c_c = ca.reshape(2, n_l // 2, 2, nt, gt, SSM_GROUP, SSM_STATE).transpose(3, 1, 0, 4, 6, 2, 5)
    w_st = c_c[:, :, :, :, :, :, None, :] * eye[None, None, None, :, None, None, :, None]
    w_st = w_st.reshape(nt, n_l // 2, 2 * TILE_STATE, 2 * LANES)

    tt = jnp.arange(n_l)[:, None]
    jj = jnp.arange(n_l)[None, :]
    lag = jnp.where(jj <= tt, tt - jj, n_l)
    k_c = jnp.concatenate([kd, jnp.zeros_like(kd[:1])], axis=0)[lag]
    k_c = k_c.reshape(n_l // 2, 2, n_l, nt, gt, SSM_GROUP, SSM_GROUP).transpose(3, 0, 2, 4, 5, 1, 6)
    w_u = k_c[:, :, :, :, :, :, None, :] * eye[None, None, None, :, None, None, :, None]
    w_u = w_u.reshape(nt, n_l // 2, n_l * LANES, 2 * LANES)

    a_l = jnp.concatenate([pw_re[n_l].reshape(nt, 1, TILE_STATE),
                           pw_im[n_l].reshape(nt, 1, TILE_STATE)], axis=-1)
    return w_in.astype(BF16), w_st.astype(BF16), w_u.astype(BF16), a_l


def _ssm_kernel(u_ref, win_ref, wst_ref, wu_ref, al_ref, y_ref, su_scr, w_scr, carry_scr):
    rows = w_scr.shape[0]
    ts = TILE_STATE

    @pl.when(pl.program_id(2) == 0)
    def _():
        carry_scr[...] = jnp.zeros_like(carry_scr)

    for j in range(SSM_L):
        su_scr[:, 2 * ts + j * LANES:2 * ts + (j + 1) * LANES] = u_ref[0, 0, j]
    w_scr[...] = jnp.dot(su_scr[:, 2 * ts:], win_ref[0], preferred_element_type=F32)

    a_re = al_ref[0, :, :ts]
    a_im = al_ref[0, :, ts:]

    def body(c, carry):
        s_re, s_im = carry
        w_re = w_scr[pl.ds(c, 1), :ts]
        w_im = w_scr[pl.ds(c, 1), ts:]
        w_scr[pl.ds(c, 1), :ts] = s_re
        w_scr[pl.ds(c, 1), ts:] = s_im
        return (a_re * s_re - a_im * s_im + w_re, a_re * s_im + a_im * s_re + w_im)

    s_re, s_im = lax.fori_loop(0, rows, body, (carry_scr[0:1, :ts], carry_scr[0:1, ts:]))
    carry_scr[0:1, :ts] = s_re
    carry_scr[0:1, ts:] = s_im

    su_scr[:, :2 * ts] = w_scr[...].astype(BF16)
    for q in range(SSM_L // 2):
        k_used = (2 * q + 2) * LANES
        yy = (jnp.dot(su_scr[:, :2 * ts], wst_ref[0, q], preferred_element_type=F32)
              + jnp.dot(su_scr[:, 2 * ts:2 * ts + k_used], wu_ref[0, q, :k_used, :],
                        preferred_element_type=F32))
        y_ref[0, 0, 2 * q] = yy[:, :LANES].astype(BF16)
        y_ref[0, 0, 2 * q + 1] = yy[:, LANES:].astype(BF16)


def _ssm(u, w_in, w_st, w_u, a_l, rows):
    bsz, _, _, nchunk, _ = u.shape
    k_all = 2 * TILE_STATE + SSM_L * LANES
    grid = (N_TILES, bsz, nchunk // rows)
    return pl.pallas_call(
        _ssm_kernel,
        grid=grid,
        in_specs=[
            pl.BlockSpec((1, 1, SSM_L, rows, LANES), lambda t, b, i: (b, t, 0, i, 0)),
            pl.BlockSpec((1, SSM_L * LANES, 2 * TILE_STATE), lambda t, b, i: (t, 0, 0)),
            pl.BlockSpec((1, SSM_L // 2, 2 * TILE_STATE, 2 * LANES), lambda t, b, i: (t, 0, 0, 0)),
            pl.BlockSpec((1, SSM_L // 2, SSM_L * LANES, 2 * LANES), lambda t, b, i: (t, 0, 0, 0)),
            pl.BlockSpec((1, 1, 2 * TILE_STATE), lambda t, b, i: (t, 0, 0)),
        ],
        out_specs=pl.BlockSpec((1, 1, SSM_L, rows, LANES), lambda t, b, i: (b, t, 0, i, 0)),
        out_shape=jax.ShapeDtypeStruct(u.shape, BF16),
        scratch_shapes=[
            pltpu.VMEM((rows, k_all), BF16),
            pltpu.VMEM((rows, 2 * TILE_STATE), F32),
            pltpu.VMEM((8, 2 * TILE_STATE), F32),
        ],
        compiler_params=_cparams(("parallel", "parallel", "arbitrary")),
        name="s5_scan",
    )(u, w_in, w_st, w_u, a_l)


def _rope(t, cos_f, sin_lo, sin_hi):
    half = ROT_DIM // 2
    width = t.shape[1]
    return (t * cos_f + pltpu.roll(t, half, 1) * sin_hi
            + pltpu.roll(t, width - half, 1) * sin_lo)


def _mid_kernel(y_ref, z_ref, x_ref, cos_ref, slo_ref, shi_ref, gluw_ref, glub_ref, outw_ref,
                kvg_ref, kvw_ref, bg_ref, binw_ref,
                h_ref, q_ref, k_ref, v_ref, z2_ref):
    y = jnp.concatenate(
        [jnp.concatenate([y_ref[0, kt, j] for kt in range(N_TILES)], axis=1) for j in range(SSM_L)],
        axis=0).astype(F32)
    y = 0.5 * y * (1.0 + lax.erf(y * (2.0 ** -0.5)))
    gate = jnp.dot(y.astype(BF16), gluw_ref[...], preferred_element_type=F32) + glub_ref[...]
    y = y * jax.nn.sigmoid(gate)
    y = y * jax.nn.silu(z_ref[0].astype(F32))
    h = _load_phase_major(x_ref) + jnp.dot(y.astype(BF16), outw_ref[...], preferred_element_type=F32)
    h_ref[0] = h

    hn = h * lax.rsqrt(jnp.mean(h * h, axis=-1, keepdims=True) + NORM_EPS)
    reps = D_MODEL // LANES
    cos_f = jnp.concatenate([cos_ref[...]] * reps, axis=1)
    sin_lo = jnp.concatenate([slo_ref[...]] * reps, axis=1)
    sin_hi = jnp.concatenate([shi_ref[...]] * reps, axis=1)

    kv = jnp.dot((hn * kvg_ref[...]).astype(BF16), kvw_ref[...], preferred_element_type=F32)
    k = _rope(kv[:, :D_MODEL], cos_f, sin_lo, sin_hi).astype(BF16)
    v = kv[:, D_MODEL:].astype(BF16)
    qz = jnp.dot((hn * bg_ref[...]).astype(BF16), binw_ref[...], preferred_element_type=F32)
    q = (_rope(qz[:, :D_MODEL], cos_f, sin_lo, sin_hi) * (HEAD_DIM ** -0.5)).astype(BF16)
    for hd in range(N_HEADS):
        cols = slice(hd * V_HEAD_DIM, (hd + 1) * V_HEAD_DIM)
        q_ref[0, hd] = q[:, cols]
        k_ref[0, hd] = k[:, cols]
        v_ref[0, hd] = v[:, cols]
    z2_ref[0] = qz[:, D_MODEL:].astype(BF16)


def _mid(y, z, x, cos_t, slo_t, shi_t, glu_w, glu_b, out_w, kv_g, kv_w, b_g, bin_w):
    bsz, nchunk, _, _ = x.shape
    seq = nchunk * SSM_L
    tok = pl.BlockSpec((1, BLK, D_MODEL), lambda b, c: (b, c, 0))
    tab = pl.BlockSpec((BLK, LANES), lambda b, c: (c, 0))
    heads = pl.BlockSpec((1, N_HEADS, BLK, V_HEAD_DIM), lambda b, c: (b, 0, c, 0))
    per_head = jax.ShapeDtypeStruct((bsz, N_HEADS, seq, V_HEAD_DIM), BF16)
    return pl.pallas_call(
        _mid_kernel,
        grid=(bsz, seq // BLK),
        in_specs=[
            pl.BlockSpec((1, N_TILES, SSM_L, BLK_CHUNKS, LANES), lambda b, c: (b, 0, 0, c, 0)),
            tok, _NATURAL_TOK, tab, tab, tab,
            _const_spec((D_MODEL, D_MODEL)), _const_spec((1, D_MODEL)), _const_spec((D_MODEL, D_MODEL)),
            _const_spec((1, D_MODEL)), _const_spec((D_MODEL, 2 * D_MODEL)),
            _const_spec((1, D_MODEL)), _const_spec((D_MODEL, 2 * D_MODEL)),
        ],
        out_specs=[tok, heads, heads, heads, tok],
        out_shape=[jax.ShapeDtypeStruct((bsz, seq, D_MODEL), F32), per_head, per_head, per_head,
                   jax.ShapeDtypeStruct((bsz, seq, D_MODEL), BF16)],
        compiler_params=_cparams(("parallel", "parallel")),
        name="s5_out_qkv",
    )(y, z, x, cos_t, slo_t, shi_t, glu_w, glu_b, out_w, kv_g, kv_w, b_g, bin_w)


V_ROWS = V_HEAD_DIM + 16


def _attn_kernel(q_ref, k_ref, v_ref, lam_ref, g_ref, o_ref,
                 qs_scr, vt_scr, sa_scr, sb_scr, m_scr, acc_scr, *, lam_init):
    tq = q_ref.shape[2]
    nblk = k_ref.shape[2] // tq
    qi = pl.program_id(2)

    @pl.when(qi == 0)
    def _():
        for c in range(nblk):
            vt = v_ref[0, 0, c * tq:(c + 1) * tq, :].astype(F32).T
            vt_scr[c, :V_HEAD_DIM, :] = vt.astype(BF16)
            row = lax.broadcasted_iota(jnp.int32, (V_ROWS - V_HEAD_DIM, tq), 0)
            vt_scr[c, V_HEAD_DIM:, :] = jnp.where(row == 0, 1.0, 0.0).astype(BF16)

    q = q_ref[0, 0]
    lane = lax.broadcasted_iota(jnp.int32, q.shape, 1)
    zero = jnp.zeros_like(q)
    qs_scr[:tq] = jnp.where(lane < HEAD_DIM, q, zero)
    qs_scr[tq:] = jnp.where(lane >= HEAD_DIM, q, zero)
    m_scr[...] = jnp.full_like(m_scr, MASK_VALUE)
    acc_scr[...] = jnp.zeros_like(acc_scr)

    def scores(kb, s_scr):
        start = pl.multiple_of(kb * tq, tq)
        k = k_ref[0, 0, pl.ds(start, tq), :]
        s_scr[...] = lax.dot_general(k, qs_scr[...], (((1,), (1,)), ((), ())),
                                     preferred_element_type=F32)

    def update(kb, s_scr, masked):
        s = s_scr[...]
        if masked:
            key = _block_position(lax.broadcasted_iota(jnp.int32, s.shape, 0))
            qry = _block_position(lax.broadcasted_iota(jnp.int32, s.shape, 1) & (tq - 1))
            s = jnp.where(key <= qry, s, MASK_VALUE)
        m_prev = m_scr[0:1, :]
        m_new = jnp.maximum(m_prev, jnp.max(s, axis=0, keepdims=True))
        alpha = jnp.exp(m_prev - m_new)
        p = jnp.exp(s - m_new).astype(BF16)
        pv = jnp.dot(vt_scr[kb], p, preferred_element_type=F32)
        acc_scr[...] = alpha * acc_scr[...] + pv
        m_scr[0:1, :] = m_new

    scores(0, sa_scr)
    npairs = qi // 2

    def pair(i, carry):
        kb = 2 * i
        scores(kb + 1, sb_scr)
        update(kb, sa_scr, False)
        scores(kb + 2, sa_scr)
        update(kb + 1, sb_scr, False)
        return carry

    lax.fori_loop(0, npairs, pair, 0)
    odd = qi - 2 * npairs

    @pl.when(odd == 0)
    def _():
        update(qi, sa_scr, True)

    @pl.when(odd == 1)
    def _():
        scores(qi, sb_scr)
        update(qi - 1, sa_scr, False)
        update(qi, sb_scr, True)

    acc = acc_scr[...]
    o_all = acc[:V_HEAD_DIM] / acc[V_HEAD_DIM:V_HEAD_DIM + 1]
    o = o_all[:, :tq] - lam_ref[:, 0:1] * o_all[:, tq:]
    o = o * lax.rsqrt(jnp.mean(o * o, axis=0, keepdims=True) + SUBLN_EPS)
    o = (o * g_ref[...]) * (1.0 - lam_init)
    o_ref[0, 0] = o.T.astype(BF16)


def _attn(q, k, v, lam_row, subln_g, lam_init):
    bsz, _, seq, _ = q.shape
    tq = BLK
    tile = pl.BlockSpec((1, 1, tq, V_HEAD_DIM), lambda b, h, i: (b, h, i, 0))
    whole = pl.BlockSpec((1, 1, seq, V_HEAD_DIM), lambda b, h, i: (b, h, 0, 0))
    return pl.pallas_call(
        functools.partial(_attn_kernel, lam_init=lam_init),
        grid=(bsz, N_HEADS, seq // tq),
        in_specs=[tile, whole, whole, _const_spec((1, V_HEAD_DIM)), _const_spec((V_HEAD_DIM, 1))],
        out_specs=tile,
        out_shape=jax.ShapeDtypeStruct(q.shape, BF16),
        scratch_shapes=[
            pltpu.VMEM((2 * tq, V_HEAD_DIM), BF16),
            pltpu.VMEM((seq // tq, V_ROWS, tq), BF16),
            pltpu.VMEM((tq, 2 * tq), F32),
            pltpu.VMEM((tq, 2 * tq), F32),
            pltpu.VMEM((8, 2 * tq), F32),
            pltpu.VMEM((V_ROWS, 2 * tq), F32),
        ],
        compiler_params=_cparams(("parallel", "parallel", "arbitrary")),
        name="diff_attn",
    )(q, k, v, lam_row, subln_g)


def _final_kernel(o_ref, z_ref, h_ref, w_ref, g_ref, out_ref):
    o = jnp.concatenate([o_ref[0, hd] for hd in range(N_HEADS)], axis=1)
    y = o.astype(F32) * jax.nn.silu(z_ref[0].astype(F32))
    h = h_ref[0] + jnp.dot(y.astype(BF16), w_ref[...], preferred_element_type=F32)
    out = h * lax.rsqrt(jnp.mean(h * h, axis=-1, keepdims=True) + NORM_EPS) * g_ref[...]
    for j in range(SSM_L):
        out_ref[0, :, j, :] = out[j * BLK_CHUNKS:(j + 1) * BLK_CHUNKS]


def _final(o, z2, h1, out_w, g):
    bsz, seq, _ = h1.shape
    tok = pl.BlockSpec((1, BLK, D_MODEL), lambda b, i: (b, i, 0))
    return pl.pallas_call(
        _final_kernel,
        grid=(bsz, seq // BLK),
        in_specs=[pl.BlockSpec((1, N_HEADS, BLK, V_HEAD_DIM), lambda b, i: (b, 0, i, 0)), tok, tok,
                  _const_spec((D_MODEL, D_MODEL)), _const_spec((1, D_MODEL))],
        out_specs=_NATURAL_TOK,
        out_shape=jax.ShapeDtypeStruct((bsz, seq // SSM_L, SSM_L, D_MODEL), F32),
        compiler_params=_cparams(("parallel", "parallel")),
        name="attn_out_final",
    )(o, z2, h1, out_w, g).reshape(bsz, seq, D_MODEL)


def _rope_tables(seq):
    half = ROT_DIM // 2
    inv = ROPE_THETA ** (-jnp.arange(0, ROT_DIM, 2, dtype=F32) / ROT_DIM)
    row = jnp.arange(seq, dtype=jnp.int32)
    pos = (row // BLK) * BLK + _block_position(row % BLK)
    ang = pos.astype(F32)[:, None] * inv[None, :]
    cos, sin = jnp.cos(ang), jnp.sin(ang)
    ones = jnp.ones((seq, HEAD_DIM - ROT_DIM), F32)
    zeros = jnp.zeros((seq, HEAD_DIM - ROT_DIM), F32)
    zh = jnp.zeros((seq, half), F32)
    cos_h = jnp.concatenate([cos, cos, ones], axis=1)
    lo_h = jnp.concatenate([-sin, zh, zeros], axis=1)
    hi_h = jnp.concatenate([zh, sin, zeros], axis=1)
    rep = LANES // HEAD_DIM
    return jnp.tile(cos_h, (1, rep)), jnp.tile(lo_h, (1, rep)), jnp.tile(hi_h, (1, rep))


def _pick(n, want):
    r = min(n, want)
    assert n % r == 0, (n, want)
    return r


def kernel(x, a_norm_g, a_in_w, a_lambda_re, a_lambda_im, a_log_dt, a_b_re, a_b_im, a_c_re, a_c_im, a_d, a_glu_w, a_glu_b, a_out_w, kv_norm_g, kv_w, b_norm_g, b_in_w, b_lambda_q1, b_lambda_k1, b_lambda_q2, b_lambda_k2, b_subln_g, b_out_w, final_norm_g):
    bsz, seq, d = x.shape
    assert d == D_MODEL and seq % BLK == 0
    nchunk = seq // SSM_L
    x = x.reshape(bsz, nchunk, SSM_L, D_MODEL)

    u, z = _inproj(x, a_norm_g[0][None], a_in_w[0].astype(BF16))
    w_in, w_st, w_u, a_l = _ssm_tables(a_lambda_re[0], a_lambda_im[0], a_log_dt[0], a_b_re[0],
                                       a_b_im[0], a_c_re[0], a_c_im[0], a_d[0])
    y = _ssm(u, w_in, w_st, w_u, a_l, _pick(nchunk, 512))

    cos_t, slo_t, shi_t = _rope_tables(seq)
    h1, q, k, v, z2 = _mid(y, z, x, cos_t, slo_t, shi_t,
                           a_glu_w[0].astype(BF16), a_glu_b[0][None], a_out_w[0].astype(BF16),
                           kv_norm_g[None], kv_w.astype(BF16),
                           b_norm_g[0][None], b_in_w[0].astype(BF16))

    lam_init = 0.8 - 0.6 * math.exp(-0.3 * N_A_LAYERS)
    lam = (jnp.exp(jnp.sum(b_lambda_q1[0] * b_lambda_k1[0]))
           - jnp.exp(jnp.sum(b_lambda_q2[0] * b_lambda_k2[0])) + lam_init)
    lam_row = jnp.full((1, V_HEAD_DIM), lam, F32)
    o = _attn(q, k, v, lam_row, b_subln_g[0][:, None], lam_init)

    return _final(o, z2, h1, b_out_w[0].astype(BF16), final_norm_g[None])
```

```python
import functools
import math

import jax
import jax.numpy as jnp
import numpy as np
from jax import lax
from jax.experimental import pallas as pl
from jax.experimental.pallas import tpu as pltpu

F32 = jnp.float32
BF16 = jnp.bfloat16

D_MODEL = 1024
SSM_GROUPS = 64
SSM_GROUP = 16
SSM_STATE = 64
N_HEADS = 8
HEAD_DIM = 64
V_HEAD_DIM = 128
ROT_DIM = 16
ROPE_THETA = 500000.0
NORM_EPS = 1e-6
SUBLN_EPS = 1e-5
N_A_LAYERS = 1

LANES = 128
SSM_L = 8
GROUPS_PER_TILE = LANES // SSM_GROUP
N_TILES = D_MODEL // LANES
TILE_STATE = GROUPS_PER_TILE * SSM_STATE
BLK = 512
BLK_CHUNKS = BLK // SSM_L
MASK_VALUE = -1e30
VMEM_LIMIT = 56 * 1024 * 1024


def _cparams(sem):
    return pltpu.CompilerParams(dimension_semantics=sem, vmem_limit_bytes=VMEM_LIMIT)


def _const_spec(shape):
    return pl.BlockSpec(shape, lambda *_: (0,) * len(shape), pipeline_mode=pl.Buffered(1))


def _load_phase_major(x_ref):
    return jnp.concatenate([x_ref[0, :, j, :] for j in range(SSM_L)], axis=0)


_NATURAL_TOK = pl.BlockSpec((1, BLK_CHUNKS, SSM_L, D_MODEL), lambda b, c: (b, c, 0, 0))


def _block_position(row):
    return (row % BLK_CHUNKS) * SSM_L + row // BLK_CHUNKS


def _inproj_kernel(x_ref, g_ref, w_ref, u_ref, z_ref):
    x = _load_phase_major(x_ref)
    xn = x * lax.rsqrt(jnp.mean(x * x, axis=-1, keepdims=True) + NORM_EPS) * g_ref[...]
    uz = jnp.dot(xn.astype(BF16), w_ref[...], preferred_element_type=F32)
    for j in range(SSM_L):
        rows = slice(j * BLK_CHUNKS, (j + 1) * BLK_CHUNKS)
        for kt in range(N_TILES):
            u_ref[0, kt, j] = uz[rows, kt * LANES:(kt + 1) * LANES].astype(BF16)
    z_ref[0] = uz[:, D_MODEL:].astype(BF16)


def _inproj(x, g, w):
    bsz, nchunk, _, _ = x.shape
    seq = nchunk * SSM_L
    tok = pl.BlockSpec((1, BLK, D_MODEL), lambda b, c: (b, c, 0))
    return pl.pallas_call(
        _inproj_kernel,
        grid=(bsz, seq // BLK),
        in_specs=[_NATURAL_TOK, _const_spec((1, D_MODEL)), _const_spec((D_MODEL, 2 * D_MODEL))],
        out_specs=[
            pl.BlockSpec((1, N_TILES, SSM_L, BLK_CHUNKS, LANES), lambda b, c: (b, 0, 0, c, 0)),
            tok,
        ],
        out_shape=[
            jax.ShapeDtypeStruct((bsz, N_TILES, SSM_L, nchunk, LANES), BF16),
            jax.ShapeDtypeStruct((bsz, seq, D_MODEL), BF16),
        ],
        compiler_params=_cparams(("parallel", "parallel")),
        name="s5_inproj",
    )(x, g, w)


def _ssm_tables(lam_re, lam_im, log_dt, b_re, b_im, c_re, c_im, d_skip):
    hi = lax.Precision.HIGHEST
    n_l = SSM_L
    dt = jnp.exp(log_dt)[:, None]
    lr = lam_re * dt
    li = lam_im * dt
    mag = jnp.exp(lr)
    abar_re = mag * jnp.cos(li)
    abar_im = mag * jnp.sin(li)
    den = lam_re * lam_re + lam_im * lam_im
    nr = abar_re - 1.0
    ni = abar_im
    f_re = (nr * lam_re + ni * lam_im) / den
    f_im = (ni * lam_re - nr * lam_im) / den
    bbar_re = f_re[..., None] * b_re - f_im[..., None] * b_im
    bbar_im = f_re[..., None] * b_im + f_im[..., None] * b_re
    n = jnp.arange(n_l + 1, dtype=F32)[:, None, None]
    pmag = jnp.exp(n * lr[None])
    pw_re = pmag * jnp.cos(n * li[None])
    pw_im = pmag * jnp.sin(n * li[None])
    ab_re = pw_re[..., None] * bbar_re[None] - pw_im[..., None] * bbar_im[None]
    ab_im = pw_re[..., None] * bbar_im[None] + pw_im[..., None] * bbar_re[None]
    kd = (jnp.einsum('gop,tgpi->tgio', c_re, ab_re[:n_l], precision=hi)
          - jnp.einsum('gop,tgpi->tgio', c_im, ab_im[:n_l], precision=hi))
    d_diag = d_skip.reshape(SSM_GROUPS, SSM_GROUP)[:, :, None] * jnp.eye(SSM_GROUP, dtype=F32)
    kd = kd.at[0].add(d_diag)
    ca_re = c_re[None] * pw_re[1:, :, None, :] - c_im[None] * pw_im[1:, :, None, :]
    ca_im = c_re[None] * pw_im[1:, :, None, :] + c_im[None] * pw_re[1:, :, None, :]

    gt, nt = GROUPS_PER_TILE, N_TILES
    n_q = n_l // 2
    ab = jnp.flip(jnp.stack([ab_re[:n_l], ab_im[:n_l]], axis=0), axis=1)
    a_c = ab.reshape(2, n_l, nt, gt, SSM_STATE, SSM_GROUP).transpose(2, 1, 3, 5, 0, 4)
    a_c = a_c.reshape(nt, n_l * LANES, 2 * SSM_STATE)
    ca = jnp.stack([ca_re, -ca_im], axis=0)
    c_c = ca.reshape(2, n_q, 2, nt, gt, SSM_GROUP, SSM_STATE).transpose(3, 0, 4, 6, 1, 2, 5)
    c_c = c_c.reshape(nt, 2 * TILE_STATE, n_l * SSM_GROUP)
    tt = jnp.arange(n_l)[:, None]
    jj = jnp.arange(n_l)[None, :]
    lag = jnp.where(jj <= tt, tt - jj, n_l)
    k_c = jnp.concatenate([kd, jnp.zeros_like(kd[:1])], axis=0)[lag]
    k_c = k_c.reshape(n_q, 2, n_l, nt, gt, SSM_GROUP, SSM_GROUP).transpose(3, 2, 4, 5, 0, 1, 6)
    k_c = k_c.reshape(nt, n_l * LANES, n_l * SSM_GROUP)
    a_l = jnp.concatenate([pw_re[n_l].reshape(nt, 1, TILE_STATE),
                           pw_im[n_l].reshape(nt, 1, TILE_STATE)], axis=-1)
    return a_c.astype(BF16), c_c.astype(BF16), k_c.astype(BF16), a_l


def _selection_matrices():
    r = np.arange(2 * SSM_STATE)[:, None]
    c = np.arange(2 * TILE_STATE)[None, :]
    e_in = (r // SSM_STATE == c // TILE_STATE) & (r % SSM_STATE == c % SSM_STATE)
    r = np.arange(SSM_L * SSM_GROUP)[:, None]
    c = np.arange(2 * LANES)[None, :]
    same = ((r // SSM_GROUP) % 2 == c // LANES) & (r % SSM_GROUP == c % SSM_GROUP)
    e_q = np.stack([same & (r // (2 * SSM_GROUP) == q) for q in range(SSM_L // 2)])
    return jnp.asarray(e_in, BF16), jnp.asarray(e_q, BF16)


def _expand_groups(compact, sel, row_group_width, col_group_width):
    full = jnp.dot(compact, sel, preferred_element_type=F32)
    row = lax.broadcasted_iota(jnp.int32, full.shape, 0)
    col = lax.broadcasted_iota(jnp.int32, full.shape, 1)
    gi = (row >> (row_group_width.bit_length() - 1)) & (GROUPS_PER_TILE - 1)
    gj = (col >> (col_group_width.bit_length() - 1)) & (GROUPS_PER_TILE - 1)
    return jnp.where(gi == gj, full, 0.0).astype(BF16)


def _ssm_kernel(u_ref, ac_ref, cc_ref, kc_ref, ein_ref, eq_ref, al_ref, y_ref,
                su_scr, w_scr, carry_scr, win_scr, wst_scr, wu_scr):
    rows = w_scr.shape[0]
    ts = TILE_STATE

    @pl.when((pl.program_id(1) == 0) & (pl.program_id(2) == 0))
    def _():
        win_scr[...] = _expand_groups(ac_ref[0], ein_ref[...], SSM_GROUP, SSM_STATE)
        for q in range(SSM_L // 2):
            wst_scr[q] = _expand_groups(cc_ref[0], eq_ref[q], SSM_STATE, SSM_GROUP)
            wu_scr[q] = _expand_groups(kc_ref[0], eq_ref[q], SSM_GROUP, SSM_GROUP)

    @pl.when(pl.program_id(2) == 0)
    def _():
        carry_scr[...] = jnp.zeros_like(carry_scr)

    for j in range(SSM_L):
        su_scr[:, 2 * ts + j * LANES:2 * ts + (j + 1) * LANES] = u_ref[0, 0, j]
    w_scr[...] = jnp.dot(su_scr[:, 2 * ts:], win_scr[...], preferred_element_type=F32)

    a_re = al_ref[0, :, :ts]
    a_im = al_ref[0, :, ts:]

    def body(c, carry):
        s_re, s_im = carry
        w_re = w_scr[pl.ds(c, 1), :ts]
        w_im = w_scr[pl.ds(c, 1), ts:]
        w_scr[pl.ds(c, 1), :ts] = s_re
        w_scr[pl.ds(c, 1), ts:] = s_im
        return (a_re * s_re - a_im * s_im + w_re, a_re * s_im + a_im * s_re + w_im)

    s_re, s_im = lax.fori_loop(0, rows, body, (carry_scr[0:1, :ts], carry_scr[0:1, ts:]))
    carry_scr[0:1, :ts] = s_re
    carry_scr[0:1, ts:] = s_im

    su_scr[:, :2 * ts] = w_scr[...].astype(BF16)
    for q in range(SSM_L // 2):
        k_used = (2 * q + 2) * LANES
        yy = (jnp.dot(su_scr[:, :2 * ts], wst_scr[q], preferred_element_type=F32)
              + jnp.dot(su_scr[:, 2 * ts:2 * ts + k_used], wu_scr[q, :k_used, :],
                        preferred_element_type=F32))
        y_ref[0, 0, 2 * q] = yy[:, :LANES].astype(BF16)
        y_ref[0, 0, 2 * q + 1] = yy[:, LANES:].astype(BF16)


def _ssm(u, a_c, c_c, k_c, a_l, rows):
    bsz, _, _, nchunk, _ = u.shape
    k_all = 2 * TILE_STATE + SSM_L * LANES
    grid = (N_TILES, bsz, nchunk // rows)
    return pl.pallas_call(
        _ssm_kernel,
        grid=grid,
        in_specs=[
            pl.BlockSpec((1, 1, SSM_L, rows, LANES), lambda t, b, i: (b, t, 0, i, 0)),
            pl.BlockSpec((1, SSM_L * LANES, 2 * SSM_STATE), lambda t, b, i: (t, 0, 0)),
            pl.BlockSpec((1, 2 * TILE_STATE, SSM_L * SSM_GROUP), lambda t, b, i: (t, 0, 0)),
            pl.BlockSpec((1, SSM_L * LANES, SSM_L * SSM_GROUP), lambda t, b, i: (t, 0, 0)),
            _const_spec((2 * SSM_STATE, 2 * TILE_STATE)),
            _const_spec((SSM_L // 2, SSM_L * SSM_GROUP, 2 * LANES)),
            pl.BlockSpec((1, 1, 2 * TILE_STATE), lambda t, b, i: (t, 0, 0)),
        ],
        out_specs=pl.BlockSpec((1, 1, SSM_L, rows, LANES), lambda t, b, i: (b, t, 0, i, 0)),
        out_shape=jax.ShapeDtypeStruct(u.shape, BF16),
        scratch_shapes=[
            pltpu.VMEM((rows, k_all), BF16),
            pltpu.VMEM((rows, 2 * TILE_STATE), F32),
            pltpu.VMEM((8, 2 * TILE_STATE), F32),
            pltpu.VMEM((SSM_L * LANES, 2 * TILE_STATE), BF16),
            pltpu.VMEM((SSM_L // 2, 2 * TILE_STATE, 2 * LANES), BF16),
            pltpu.VMEM((SSM_L // 2, SSM_L * LANES, 2 * LANES), BF16),
        ],
        compiler_params=_cparams(("parallel", "arbitrary", "arbitrary")),
        name="s5_scan",
    )(u, a_c, c_c, k_c, *_selection_matrices(), a_l)


def _rope(t, cos_f, sin_lo, sin_hi):
    half = ROT_DIM // 2
    width = t.shape[1]
    return (t * cos_f + pltpu.roll(t, half, 1) * sin_hi
            + pltpu.roll(t, width - half, 1) * sin_lo)


def _mid_kernel(y_ref, z_ref, x_ref, cos_ref, slo_ref, shi_ref, gluw_ref, glub_ref, outw_ref,
                kvg_ref, kvw_ref, bg_ref, binw_ref,
                h_ref, q_ref, k_ref, v_ref, z2_ref):
    y = jnp.concatenate(
        [jnp.concatenate([y_ref[0, kt, j] for kt in range(N_TILES)], axis=1) for j in range(SSM_L)],
        axis=0).astype(F32)
    y = 0.5 * y * (1.0 + lax.erf(y * (2.0 ** -0.5)))
    gate = jnp.dot(y.astype(BF16), gluw_ref[...], preferred_element_type=F32) + glub_ref[...]
    y = y * jax.nn.sigmoid(gate)
    y = y * jax.nn.silu(z_ref[0].astype(F32))
    h = _load_phase_major(x_ref) + jnp.dot(y.astype(BF16), outw_ref[...], preferred_element_type=F32)
    h_ref[0] = h

    hn = h * lax.rsqrt(jnp.mean(h * h, axis=-1, keepdims=True) + NORM_EPS)
    reps = D_MODEL // LANES
    cos_f = jnp.concatenate([cos_ref[...]] * reps, axis=1)
    sin_lo = jnp.concatenate([slo_ref[...]] * reps, axis=1)
    sin_hi = jnp.concatenate([shi_ref[...]] * reps, axis=1)

    kv = jnp.dot((hn * kvg_ref[...]).astype(BF16), kvw_ref[...], preferred_element_type=F32)
    k = _rope(kv[:, :D_MODEL], cos_f, sin_lo, sin_hi).astype(BF16)
    v = kv[:, D_MODEL:].astype(BF16)
    qz = jnp.dot((hn * bg_ref[...]).astype(BF16), binw_ref[...], preferred_element_type=F32)
    q = (_rope(qz[:, :D_MODEL], cos_f, sin_lo, sin_hi) * (HEAD_DIM ** -0.5)).astype(BF16)
    for hd in range(N_HEADS):
        cols = slice(hd * V_HEAD_DIM, (hd + 1) * V_HEAD_DIM)
        q_ref[0, hd] = q[:, cols]
        k_ref[0, hd] = k[:, cols]
        v_ref[0, hd] = v[:, cols]
    z2_ref[0] = qz[:, D_MODEL:].astype(BF16)


def _mid(y, z, x, cos_t, slo_t, shi_t, glu_w, glu_b, out_w, kv_g, kv_w, b_g, bin_w):
    bsz, nchunk, _, _ = x.shape
    seq = nchunk * SSM_L
    tok = pl.BlockSpec((1, BLK, D_MODEL), lambda b, c: (b, c, 0))
    tab = pl.BlockSpec((BLK, LANES), lambda b, c: (c, 0))
    heads = pl.BlockSpec((1, N_HEADS, BLK, V_HEAD_DIM), lambda b, c: (b, 0, c, 0))
    per_head = jax.ShapeDtypeStruct((bsz, N_HEADS, seq, V_HEAD_DIM), BF16)
    return pl.pallas_call(
        _mid_kernel,
        grid=(bsz, seq // BLK),
        in_specs=[
            pl.BlockSpec((1, N_TILES, SSM_L, BLK_CHUNKS, LANES), lambda b, c: (b, 0, 0, c, 0)),
            tok, _NATURAL_TOK, tab, tab, tab,
            _const_spec((D_MODEL, D_MODEL)), _const_spec((1, D_MODEL)), _const_spec((D_MODEL, D_MODEL)),
            _const_spec((1, D_MODEL)), _const_spec((D_MODEL, 2 * D_MODEL)),
            _const_spec((1, D_MODEL)), _const_spec((D_MODEL, 2 * D_MODEL)),
        ],
        out_specs=[tok, heads, heads, heads, tok],
        out_shape=[jax.ShapeDtypeStruct((bsz, seq, D_MODEL), F32), per_head, per_head, per_head,
                   jax.ShapeDtypeStruct((bsz, seq, D_MODEL), BF16)],
        compiler_params=_cparams(("parallel", "parallel")),
        name="s5_out_qkv",
    )(y, z, x, cos_t, slo_t, shi_t, glu_w, glu_b, out_w, kv_g, kv_w, b_g, bin_w)


V_ROWS = V_HEAD_DIM + 16


def _causal_bias():
    pos = _block_position(np.arange(BLK))
    visible = pos[:, None] <= np.concatenate([pos, pos])[None, :]
    return jnp.asarray(np.where(visible, 0.0, MASK_VALUE), F32)


def _attn_kernel(q_ref, k_ref, v_ref, bias_ref, lam_ref, g_ref, o_ref,
                 qs_scr, vt_scr, sa_scr, sb_scr, m_scr, acc_scr, *, lam_init):
    tq = q_ref.shape[2]
    nblk = k_ref.shape[2] // tq
    qi = pl.program_id(2)

    @pl.when(qi == 0)
    def _():
        for c in range(nblk):
            vt = v_ref[0, 0, c * tq:(c + 1) * tq, :].astype(F32).T
            vt_scr[c, :V_HEAD_DIM, :] = vt.astype(BF16)
            row = lax.broadcasted_iota(jnp.int32, (V_ROWS - V_HEAD_DIM, tq), 0)
            vt_scr[c, V_HEAD_DIM:, :] = jnp.where(row == 0, 1.0, 0.0).astype(BF16)

    qt = q_ref[0, 0].astype(F32).T
    dim = lax.broadcasted_iota(jnp.int32, qt.shape, 0)
    qs_scr[:, :tq] = jnp.where(dim < HEAD_DIM, qt, 0.0).astype(BF16)
    qs_scr[:, tq:] = jnp.where(dim >= HEAD_DIM, qt, 0.0).astype(BF16)
    m_scr[...] = jnp.full_like(m_scr, MASK_VALUE)
    acc_scr[...] = jnp.zeros_like(acc_scr)

    def scores(kb, s_scr):
        start = pl.multiple_of(kb * tq, tq)
        k = k_ref[0, 0, pl.ds(start, tq), :]
        s_scr[...] = jnp.dot(k, qs_scr[...], preferred_element_type=F32)

    def update(kb, s_scr, masked):
        s = s_scr[...]
        if masked:
            s = s + bias_ref[...]
        m_prev = m_scr[0:1, :]
        m_new = jnp.maximum(m_prev, jnp.max(s, axis=0, keepdims=True))
        alpha = jnp.exp(m_prev - m_new)
        p = jnp.exp(s - m_new).astype(BF16)
        pv = jnp.dot(vt_scr[kb], p, preferred_element_type=F32)
        acc_scr[...] = alpha * acc_scr[...] + pv
        m_scr[0:1, :] = m_new

    scores(0, sa_scr)
    npairs = qi // 2

    def pair(i, carry):
        kb = 2 * i
        scores(kb + 1, sb_scr)
        update(kb, sa_scr, False)
        scores(kb + 2, sa_scr)
        update(kb + 1, sb_scr, False)
        return carry

    lax.fori_loop(0, npairs, pair, 0)
    odd = qi - 2 * npairs

    @pl.when(odd == 0)
    def _():
        update(qi, sa_scr, True)

    @pl.when(odd == 1)
    def _():
        scores(qi, sb_scr)
        update(qi - 1, sa_scr, False)
        update(qi, sb_scr, True)

    acc = acc_scr[...]
    o_all = acc[:V_HEAD_DIM] / acc[V_HEAD_DIM:V_HEAD_DIM + 1]
    o = o_all[:, :tq] - lam_ref[:, 0:1] * o_all[:, tq:]
    o = o * lax.rsqrt(jnp.mean(o * o, axis=0, keepdims=True) + SUBLN_EPS)
    o = (o * g_ref[...]) * (1.0 - lam_init)
    o_ref[0, 0] = o.T.astype(BF16)


def _attn(q, k, v, lam_row, subln_g, lam_init):
    bsz, _, seq, _ = q.shape
    tq = BLK
    tile = pl.BlockSpec((1, 1, tq, V_HEAD_DIM), lambda b, h, i: (b, h, i, 0))
    whole = pl.BlockSpec((1, 1, seq, V_HEAD_DIM), lambda b, h, i: (b, h, 0, 0))
    return pl.pallas_call(
        functools.partial(_attn_kernel, lam_init=lam_init),
        grid=(bsz, N_HEADS, seq // tq),
        in_specs=[tile, whole, whole, _const_spec((tq, 2 * tq)),
                  _const_spec((1, V_HEAD_DIM)), _const_spec((V_HEAD_DIM, 1))],
        out_specs=tile,
        out_shape=jax.ShapeDtypeStruct(q.shape, BF16),
        scratch_shapes=[
            pltpu.VMEM((V_HEAD_DIM, 2 * tq), BF16),
            pltpu.VMEM((seq // tq, V_ROWS, tq), BF16),
            pltpu.VMEM((tq, 2 * tq), F32),
            pltpu.VMEM((tq, 2 * tq), F32),
            pltpu.VMEM((8, 2 * tq), F32),
            pltpu.VMEM((V_ROWS, 2 * tq), F32),
        ],
        compiler_params=_cparams(("parallel", "parallel", "arbitrary")),
        name="diff_attn",
    )(q, k, v, _causal_bias(), lam_row, subln_g)


def _final_kernel(o_ref, z_ref, h_ref, w_ref, g_ref, out_ref):
    o = jnp.concatenate([o_ref[0, hd] for hd in range(N_HEADS)], axis=1)
    y = o.astype(F32) * jax.nn.silu(z_ref[0].astype(F32))
    h = h_ref[0] + jnp.dot(y.astype(BF16), w_ref[...], preferred_element_type=F32)
    out = h * lax.rsqrt(jnp.mean(h * h, axis=-1, keepdims=True) + NORM_EPS) * g_ref[...]
    for j in range(SSM_L):
        out_ref[0, :, j, :] = out[j * BLK_CHUNKS:(j + 1) * BLK_CHUNKS]


def _final(o, z2, h1, out_w, g):
    bsz, seq, _ = h1.shape
    tok = pl.BlockSpec((1, BLK, D_MODEL), lambda b, i: (b, i, 0))
    return pl.pallas_call(
        _final_kernel,
        grid=(bsz, seq // BLK),
        in_specs=[pl.BlockSpec((1, N_HEADS, BLK, V_HEAD_DIM), lambda b, i: (b, 0, i, 0)), tok, tok,
                  _const_spec((D_MODEL, D_MODEL)), _const_spec((1, D_MODEL))],
        out_specs=_NATURAL_TOK,
        out_shape=jax.ShapeDtypeStruct((bsz, seq // SSM_L, SSM_L, D_MODEL), F32),
        compiler_params=_cparams(("parallel", "parallel")),
        name="attn_out_final",
    )(o, z2, h1, out_w, g).reshape(bsz, seq, D_MODEL)


def _rope_tables(seq):
    half = ROT_DIM // 2
    inv = ROPE_THETA ** (-jnp.arange(0, ROT_DIM, 2, dtype=F32) / ROT_DIM)
    row = jnp.arange(seq, dtype=jnp.int32)
    pos = (row // BLK) * BLK + _block_position(row % BLK)
    ang = pos.astype(F32)[:, None] * inv[None, :]
    cos, sin = jnp.cos(ang), jnp.sin(ang)
    ones = jnp.ones((seq, HEAD_DIM - ROT_DIM), F32)
    zeros = jnp.zeros((seq, HEAD_DIM - ROT_DIM), F32)
    zh = jnp.zeros((seq, half), F32)
    cos_h = jnp.concatenate([cos, cos, ones], axis=1)
    lo_h = jnp.concatenate([-sin, zh, zeros], axis=1)
    hi_h = jnp.concatenate([zh, sin, zeros], axis=1)
    rep = LANES // HEAD_DIM
    return jnp.tile(cos_h, (1, rep)), jnp.tile(lo_h, (1, rep)), jnp.tile(hi_h, (1, rep))


def _pick(n, want):
    r = min(n, want)
    assert n % r == 0, (n, want)
    return r


def kernel(x, a_norm_g, a_in_w, a_lambda_re, a_lambda_im, a_log_dt, a_b_re, a_b_im, a_c_re, a_c_im, a_d, a_glu_w, a_glu_b, a_out_w, kv_norm_g, kv_w, b_norm_g, b_in_w, b_lambda_q1, b_lambda_k1, b_lambda_q2, b_lambda_k2, b_subln_g, b_out_w, final_norm_g):
    bsz, seq, d = x.shape
    assert d == D_MODEL and seq % BLK == 0
    nchunk = seq // SSM_L
    x = x.reshape(bsz, nchunk, SSM_L, D_MODEL)

    u, z = _inproj(x, a_norm_g[0][None], a_in_w[0].astype(BF16))
    a_c, c_c, k_c, a_l = _ssm_tables(a_lambda_re[0], a_lambda_im[0], a_log_dt[0], a_b_re[0],
                                     a_b_im[0], a_c_re[0], a_c_im[0], a_d[0])
    y = _ssm(u, a_c, c_c, k_c, a_l, _pick(nchunk, 512))

    cos_t, slo_t, shi_t = _rope_tables(seq)
    h1, q, k, v, z2 = _mid(y, z, x, cos_t, slo_t, shi_t,
                           a_glu_w[0].astype(BF16), a_glu_b[0][None], a_out_w[0].astype(BF16),
                           kv_norm_g[None], kv_w.astype(BF16),
                           b_norm_g[0][None], b_in_w[0].astype(BF16))

    lam_init = 0.8 - 0.6 * math.exp(-0.3 * N_A_LAYERS)
    lam = (jnp.exp(jnp.sum(b_lambda_q1[0] * b_lambda_k1[0]))
           - jnp.exp(jnp.sum(b_lambda_q2[0] * b_lambda_k2[0])) + lam_init)
    lam_row = jnp.full((1, V_HEAD_DIM), lam, F32)
    o = _attn(q, k, v, lam_row, b_subln_g[0][:, None], lam_init)

    return _final(o, z2, h1, b_out_w[0].astype(BF16), final_norm_g[None])
```

```python
import functools
import math

import jax
import jax.numpy as jnp
import numpy as np
from jax import lax
from jax.experimental import pallas as pl
from jax.experimental.pallas import tpu as pltpu

F32 = jnp.float32
BF16 = jnp.bfloat16

D_MODEL = 1024
SSM_GROUPS = 64
SSM_GROUP = 16
SSM_STATE = 64
N_HEADS = 8
HEAD_DIM = 64
V_HEAD_DIM = 128
ROT_DIM = 16
ROPE_THETA = 500000.0
NORM_EPS = 1e-6
SUBLN_EPS = 1e-5
N_A_LAYERS = 1

LANES = 128
SSM_L = 8
GROUPS_PER_TILE = LANES // SSM_GROUP
N_TILES = D_MODEL // LANES
TILE_STATE = GROUPS_PER_TILE * SSM_STATE
BLK = 512
BLK_CHUNKS = BLK // SSM_L
MASK_VALUE = -1e30
Q_SCALE = HEAD_DIM ** -0.5 * math.log2(math.e)
VMEM_LIMIT = 56 * 1024 * 1024


def _cparams(sem):
    return pltpu.CompilerParams(dimension_semantics=sem, vmem_limit_bytes=VMEM_LIMIT)


def _const_spec(shape):
    return pl.BlockSpec(shape, lambda *_: (0,) * len(shape), pipeline_mode=pl.Buffered(1))


def _load_phase_major(x_ref):
    return jnp.concatenate([x_ref[0, :, j, :] for j in range(SSM_L)], axis=0)


_NATURAL_TOK = pl.BlockSpec((1, BLK_CHUNKS, SSM_L, D_MODEL), lambda b, c: (b, c, 0, 0))


def _block_position(row):
    return (row % BLK_CHUNKS) * SSM_L + row // BLK_CHUNKS


def _inproj_kernel(x_ref, g_ref, w_ref, u_ref, z_ref):
    x = _load_phase_major(x_ref)
    xn = x * lax.rsqrt(jnp.mean(x * x, axis=-1, keepdims=True) + NORM_EPS) * g_ref[...]
    uz = jnp.dot(xn.astype(BF16), w_ref[...], preferred_element_type=F32)
    for j in range(SSM_L):
        rows = slice(j * BLK_CHUNKS, (j + 1) * BLK_CHUNKS)
        for kt in range(N_TILES):
            u_ref[0, kt, j] = uz[rows, kt * LANES:(kt + 1) * LANES].astype(BF16)
    z_ref[0] = uz[:, D_MODEL:].astype(BF16)


def _inproj(x, g, w):
    bsz, nchunk, _, _ = x.shape
    seq = nchunk * SSM_L
    tok = pl.BlockSpec((1, BLK, D_MODEL), lambda b, c: (b, c, 0))
    return pl.pallas_call(
        _inproj_kernel,
        grid=(bsz, seq // BLK),
        in_specs=[_NATURAL_TOK, _const_spec((1, D_MODEL)), _const_spec((D_MODEL, 2 * D_MODEL))],
        out_specs=[
            pl.BlockSpec((1, N_TILES, SSM_L, BLK_CHUNKS, LANES), lambda b, c: (b, 0, 0, c, 0)),
            tok,
        ],
        out_shape=[
            jax.ShapeDtypeStruct((bsz, N_TILES, SSM_L, nchunk, LANES), BF16),
            jax.ShapeDtypeStruct((bsz, seq, D_MODEL), BF16),
        ],
        compiler_params=_cparams(("parallel", "parallel")),
        name="s5_inproj",
    )(x, g, w)


def _ssm_tables(lam_re, lam_im, log_dt, b_re, b_im, c_re, c_im, d_skip):
    hi = lax.Precision.HIGHEST
    n_l = SSM_L
    dt = jnp.exp(log_dt)[:, None]
    lr = lam_re * dt
    li = lam_im * dt
    mag = jnp.exp(lr)
    abar_re = mag * jnp.cos(li)
    abar_im = mag * jnp.sin(li)
    den = lam_re * lam_re + lam_im * lam_im
    nr = abar_re - 1.0
    ni = abar_im
    f_re = (nr * lam_re + ni * lam_im) / den
    f_im = (ni * lam_re - nr * lam_im) / den
    bbar_re = f_re[..., None] * b_re - f_im[..., None] * b_im
    bbar_im = f_re[..., None] * b_im + f_im[..., None] * b_re
    n = jnp.arange(n_l + 1, dtype=F32)[:, None, None]
    pmag = jnp.exp(n * lr[None])
    pw_re = pmag * jnp.cos(n * li[None])
    pw_im = pmag * jnp.sin(n * li[None])
    ab_re = pw_re[..., None] * bbar_re[None] - pw_im[..., None] * bbar_im[None]
    ab_im = pw_re[..., None] * bbar_im[None] + pw_im[..., None] * bbar_re[None]
    kd = (jnp.einsum('gop,tgpi->tgio', c_re, ab_re[:n_l], precision=hi)
          - jnp.einsum('gop,tgpi->tgio', c_im, ab_im[:n_l], precision=hi))
    d_diag = d_skip.reshape(SSM_GROUPS, SSM_GROUP)[:, :, None] * jnp.eye(SSM_GROUP, dtype=F32)
    kd = kd.at[0].add(d_diag)
    ca_re = c_re[None] * pw_re[1:, :, None, :] - c_im[None] * pw_im[1:, :, None, :]
    ca_im = c_re[None] * pw_im[1:, :, None, :] + c_im[None] * pw_re[1:, :, None, :]

    gt, nt = GROUPS_PER_TILE, N_TILES
    n_q = n_l // 2
    ab = jnp.flip(jnp.stack([ab_re[:n_l], ab_im[:n_l]], axis=0), axis=1)
    a_c = ab.reshape(2, n_l, nt, gt, SSM_STATE, SSM_GROUP).transpose(2, 1, 3, 5, 0, 4)
    a_c = a_c.reshape(nt, n_l * LANES, 2 * SSM_STATE)
    ca = jnp.stack([ca_re, -ca_im], axis=0)
    c_c = ca.reshape(2, n_q, 2, nt, gt, SSM_GROUP, SSM_STATE).transpose(3, 0, 4, 6, 1, 2, 5)
    c_c = c_c.reshape(nt, 2 * TILE_STATE, n_l * SSM_GROUP)
    tt = jnp.arange(n_l)[:, None]
    jj = jnp.arange(n_l)[None, :]
    lag = jnp.where(jj <= tt, tt - jj, n_l)
    k_c = jnp.concatenate([kd, jnp.zeros_like(kd[:1])], axis=0)[lag]
    k_c = k_c.reshape(n_q, 2, n_l, nt, gt, SSM_GROUP, SSM_GROUP).transpose(3, 2, 4, 5, 0, 1, 6)
    k_c = k_c.reshape(nt, n_l * LANES, n_l * SSM_GROUP)
    a_l = jnp.concatenate([pw_re[n_l].reshape(nt, 1, TILE_STATE),
                           pw_im[n_l].reshape(nt, 1, TILE_STATE)], axis=-1)
    return a_c.astype(BF16), c_c.astype(BF16), k_c.astype(BF16), a_l


def _selection_matrices():
    r = np.arange(2 * SSM_STATE)[:, None]
    c = np.arange(2 * TILE_STATE)[None, :]
    e_in = (r // SSM_STATE == c // TILE_STATE) & (r % SSM_STATE == c % SSM_STATE)
    r = np.arange(SSM_L * SSM_GROUP)[:, None]
    c = np.arange(2 * LANES)[None, :]
    same = ((r // SSM_GROUP) % 2 == c // LANES) & (r % SSM_GROUP == c % SSM_GROUP)
    e_q = np.stack([same & (r // (2 * SSM_GROUP) == q) for q in range(SSM_L // 2)])
    return jnp.asarray(e_in, BF16), jnp.asarray(e_q, BF16)


def _expand_groups(compact, sel, row_group_width, col_group_width):
    full = jnp.dot(compact, sel, preferred_element_type=F32)
    row = lax.broadcasted_iota(jnp.int32, full.shape, 0)
    col = lax.broadcasted_iota(jnp.int32, full.shape, 1)
    gi = (row >> (row_group_width.bit_length() - 1)) & (GROUPS_PER_TILE - 1)
    gj = (col >> (col_group_width.bit_length() - 1)) & (GROUPS_PER_TILE - 1)
    return jnp.where(gi == gj, full, 0.0).astype(BF16)


def _ssm_kernel(u_ref, ac_ref, cc_ref, kc_ref, ein_ref, eq_ref, al_ref, y_ref,
                su_scr, w_scr, carry_scr, win_scr, wst_scr, wu_scr):
    rows = w_scr.shape[0]
    ts = TILE_STATE

    @pl.when((pl.program_id(1) == 0) & (pl.program_id(2) == 0))
    def _():
        win_scr[...] = _expand_groups(ac_ref[0], ein_ref[...], SSM_GROUP, SSM_STATE)
        for q in range(SSM_L // 2):
            wst_scr[q] = _expand_groups(cc_ref[0], eq_ref[q], SSM_STATE, SSM_GROUP)
            wu_scr[q] = _expand_groups(kc_ref[0], eq_ref[q], SSM_GROUP, SSM_GROUP)

    @pl.when(pl.program_id(2) == 0)
    def _():
        carry_scr[...] = jnp.zeros_like(carry_scr)

    for j in range(SSM_L):
        su_scr[:, 2 * ts + j * LANES:2 * ts + (j + 1) * LANES] = u_ref[0, 0, j]
    w_scr[...] = jnp.dot(su_scr[:, 2 * ts:], win_scr[...], preferred_element_type=F32)

    a_re = al_ref[0, :, :ts]
    a_im = al_ref[0, :, ts:]

    def body(c, carry):
        s_re, s_im = carry
        w_re = w_scr[pl.ds(c, 1), :ts]
        w_im = w_scr[pl.ds(c, 1), ts:]
        w_scr[pl.ds(c, 1), :ts] = s_re
        w_scr[pl.ds(c, 1), ts:] = s_im
        return (a_re * s_re - a_im * s_im + w_re, a_re * s_im + a_im * s_re + w_im)

    s_re, s_im = lax.fori_loop(0, rows, body, (carry_scr[0:1, :ts], carry_scr[0:1, ts:]))
    carry_scr[0:1, :ts] = s_re
    carry_scr[0:1, ts:] = s_im

    su_scr[:, :2 * ts] = w_scr[...].astype(BF16)
    for q in range(SSM_L // 2):
        k_used = (2 * q + 2) * LANES
        yy = (jnp.dot(su_scr[:, :2 * ts], wst_scr[q], preferred_element_type=F32)
              + jnp.dot(su_scr[:, 2 * ts:2 * ts + k_used], wu_scr[q, :k_used, :],
                        preferred_element_type=F32))
        y_ref[0, 0, 2 * q] = yy[:, :LANES].astype(BF16)
        y_ref[0, 0, 2 * q + 1] = yy[:, LANES:].astype(BF16)


def _ssm(u, a_c, c_c, k_c, a_l, rows):
    bsz, _, _, nchunk, _ = u.shape
    k_all = 2 * TILE_STATE + SSM_L * LANES
    grid = (N_TILES, bsz, nchunk // rows)
    return pl.pallas_call(
        _ssm_kernel,
        grid=grid,
        in_specs=[
            pl.BlockSpec((1, 1, SSM_L, rows, LANES), lambda t, b, i: (b, t, 0, i, 0)),
            pl.BlockSpec((1, SSM_L * LANES, 2 * SSM_STATE), lambda t, b, i: (t, 0, 0)),
            pl.BlockSpec((1, 2 * TILE_STATE, SSM_L * SSM_GROUP), lambda t, b, i: (t, 0, 0)),
            pl.BlockSpec((1, SSM_L * LANES, SSM_L * SSM_GROUP), lambda t, b, i: (t, 0, 0)),
            _const_spec((2 * SSM_STATE, 2 * TILE_STATE)),
            _const_spec((SSM_L // 2, SSM_L * SSM_GROUP, 2 * LANES)),
            pl.BlockSpec((1, 1, 2 * TILE_STATE), lambda t, b, i: (t, 0, 0)),
        ],
        out_specs=pl.BlockSpec((1, 1, SSM_L, rows, LANES), lambda t, b, i: (b, t, 0, i, 0)),
        out_shape=jax.ShapeDtypeStruct(u.shape, BF16),
        scratch_shapes=[
            pltpu.VMEM((rows, k_all), BF16),
            pltpu.VMEM((rows, 2 * TILE_STATE), F32),
            pltpu.VMEM((8, 2 * TILE_STATE), F32),
            pltpu.VMEM((SSM_L * LANES, 2 * TILE_STATE), BF16),
            pltpu.VMEM((SSM_L // 2, 2 * TILE_STATE, 2 * LANES), BF16),
            pltpu.VMEM((SSM_L // 2, SSM_L * LANES, 2 * LANES), BF16),
        ],
        compiler_params=_cparams(("parallel", "arbitrary", "arbitrary")),
        name="s5_scan",
    )(u, a_c, c_c, k_c, *_selection_matrices(), a_l)


def _rope(t, cos_f, sin_lo, sin_hi):
    half = ROT_DIM // 2
    width = t.shape[1]
    return (t * cos_f + pltpu.roll(t, half, 1) * sin_hi
            + pltpu.roll(t, width - half, 1) * sin_lo)


def _mid_kernel(y_ref, z_ref, x_ref, cos_ref, slo_ref, shi_ref, gluw_ref, glub_ref, outw_ref,
                kvg_ref, kvw_ref, bg_ref, binw_ref,
                h_ref, q_ref, k_ref, v_ref, z2_ref):
    y = jnp.concatenate(
        [jnp.concatenate([y_ref[0, kt, j] for kt in range(N_TILES)], axis=1) for j in range(SSM_L)],
        axis=0).astype(F32)
    y = 0.5 * y * (1.0 + lax.erf(y * (2.0 ** -0.5)))
    gate = jnp.dot(y.astype(BF16), gluw_ref[...], preferred_element_type=F32) + glub_ref[...]
    y = y * jax.nn.sigmoid(gate)
    y = y * jax.nn.silu(z_ref[0].astype(F32))
    h = _load_phase_major(x_ref) + jnp.dot(y.astype(BF16), outw_ref[...], preferred_element_type=F32)
    h_ref[0] = h

    hn = h * lax.rsqrt(jnp.mean(h * h, axis=-1, keepdims=True) + NORM_EPS)
    reps = D_MODEL // LANES
    cos_f = jnp.concatenate([cos_ref[...]] * reps, axis=1)
    sin_lo = jnp.concatenate([slo_ref[...]] * reps, axis=1)
    sin_hi = jnp.concatenate([shi_ref[...]] * reps, axis=1)

    kv = jnp.dot((hn * kvg_ref[...]).astype(BF16), kvw_ref[...], preferred_element_type=F32)
    k = _rope(kv[:, :D_MODEL], cos_f, sin_lo, sin_hi).astype(BF16)
    v = kv[:, D_MODEL:].astype(BF16)
    qz = jnp.dot((hn * bg_ref[...]).astype(BF16), binw_ref[...], preferred_element_type=F32)
    q = (_rope(qz[:, :D_MODEL], cos_f, sin_lo, sin_hi) * Q_SCALE).astype(BF16)
    for hd in range(N_HEADS):
        cols = slice(hd * V_HEAD_DIM, (hd + 1) * V_HEAD_DIM)
        q_ref[0, hd] = q[:, cols]
        k_ref[0, hd] = k[:, cols]
        v_ref[0, hd] = v[:, cols]
    z2_ref[0] = qz[:, D_MODEL:].astype(BF16)


def _mid(y, z, x, cos_t, slo_t, shi_t, glu_w, glu_b, out_w, kv_g, kv_w, b_g, bin_w):
    bsz, nchunk, _, _ = x.shape
    seq = nchunk * SSM_L
    tok = pl.BlockSpec((1, BLK, D_MODEL), lambda b, c: (b, c, 0))
    tab = pl.BlockSpec((BLK, LANES), lambda b, c: (c, 0))
    heads = pl.BlockSpec((1, N_HEADS, BLK, V_HEAD_DIM), lambda b, c: (b, 0, c, 0))
    per_head = jax.ShapeDtypeStruct((bsz, N_HEADS, seq, V_HEAD_DIM), BF16)
    return pl.pallas_call(
        _mid_kernel,
        grid=(bsz, seq // BLK),
        in_specs=[
            pl.BlockSpec((1, N_TILES, SSM_L, BLK_CHUNKS, LANES), lambda b, c: (b, 0, 0, c, 0)),
            tok, _NATURAL_TOK, tab, tab, tab,
            _const_spec((D_MODEL, D_MODEL)), _const_spec((1, D_MODEL)), _const_spec((D_MODEL, D_MODEL)),
            _const_spec((1, D_MODEL)), _const_spec((D_MODEL, 2 * D_MODEL)),
            _const_spec((1, D_MODEL)), _const_spec((D_MODEL, 2 * D_MODEL)),
        ],
        out_specs=[tok, heads, heads, heads, tok],
        out_shape=[jax.ShapeDtypeStruct((bsz, seq, D_MODEL), F32), per_head, per_head, per_head,
                   jax.ShapeDtypeStruct((bsz, seq, D_MODEL), BF16)],
        compiler_params=_cparams(("parallel", "parallel")),
        name="s5_out_qkv",
    )(y, z, x, cos_t, slo_t, shi_t, glu_w, glu_b, out_w, kv_g, kv_w, b_g, bin_w)


V_ROWS = V_HEAD_DIM + 16


def _causal_bias():
    pos = _block_position(np.arange(BLK))
    visible = pos[:, None] <= np.concatenate([pos, pos])[None, :]
    return jnp.asarray(np.where(visible, 0.0, MASK_VALUE), F32)


def _attn_kernel(q_ref, k_ref, v_ref, bias_ref, lam_ref, g_ref, o_ref,
                 qs_scr, vt_scr, sa_scr, sb_scr, m_scr, acc_scr, *, lam_init):
    tq = BLK
    nblk = k_ref.shape[2] // tq

    for c in range(nblk):
        rows = slice(c * tq, (c + 1) * tq)
        vt_scr[c, :V_HEAD_DIM, :] = v_ref[0, 0, rows, :].astype(F32).T.astype(BF16)
        one_row = lax.broadcasted_iota(jnp.int32, (V_ROWS - V_HEAD_DIM, tq), 0) == 0
        vt_scr[c, V_HEAD_DIM:, :] = jnp.where(one_row, 1.0, 0.0).astype(BF16)
        qt = q_ref[0, 0, rows, :].astype(F32).T
        dim = lax.broadcasted_iota(jnp.int32, qt.shape, 0)
        qs_scr[c, :, :tq] = jnp.where(dim < HEAD_DIM, qt, 0.0).astype(BF16)
        qs_scr[c, :, tq:] = jnp.where(dim >= HEAD_DIM, qt, 0.0).astype(BF16)
    m_scr[...] = jnp.full_like(m_scr, MASK_VALUE)
    acc_scr[...] = jnp.zeros_like(acc_scr)

    def scores(qi, kb, s_scr, diagonal):
        start = pl.multiple_of(kb * tq, tq)
        k = k_ref[0, 0, pl.ds(start, tq), :]
        s = jnp.dot(k, qs_scr[qi], preferred_element_type=F32)
        s_scr[...] = s + bias_ref[...] if diagonal else s

    def update(qi, kb, s_scr):
        s = s_scr[...]
        m_prev = m_scr[qi, 0:1, :]
        m_new = jnp.maximum(m_prev, jnp.max(s, axis=0, keepdims=True))
        alpha = jnp.exp2(m_prev - m_new)
        p = jnp.exp2(s - m_new).astype(BF16)
        pv = jnp.dot(vt_scr[kb], p, preferred_element_type=F32)
        acc_scr[qi] = alpha * acc_scr[qi] + pv
        m_scr[qi, 0:1, :] = m_new

    def walk(count, first, following, diagonal):
        if count == 0:
            return
        last_blk = nblk - 1
        scores(first[0], first[1], sa_scr, diagonal)

        def two_steps(_, carry):
            qi0, kb0 = carry
            qi1, kb1 = following(qi0, kb0)
            scores(qi1, kb1, sb_scr, diagonal)
            update(qi0, kb0, sa_scr)
            qi2, kb2 = following(qi1, kb1)
            scores(jnp.minimum(qi2, last_blk), jnp.minimum(kb2, last_blk), sa_scr, diagonal)
            update(qi1, kb1, sb_scr)
            return qi2, kb2

        qi_l, kb_l = lax.fori_loop(0, count // 2, two_steps, (jnp.int32(first[0]), jnp.int32(first[1])))
        if count % 2:
            update(qi_l, kb_l, sa_scr)

    def below_next(qi, kb):
        last = kb == qi - 1
        return jnp.where(last, qi + 1, qi), jnp.where(last, 0, kb + 1)

    walk(nblk * (nblk - 1) // 2, (1, 0), below_next, False)
    walk(nblk, (0, 0), lambda qi, kb: (qi + 1, kb + 1), True)

    def finish(qi, carry):
        acc = acc_scr[qi]
        o_all = acc[:V_HEAD_DIM] / acc[V_HEAD_DIM:V_HEAD_DIM + 1]
        o = o_all[:, :tq] - lam_ref[:, 0:1] * o_all[:, tq:]
        o = o * lax.rsqrt(jnp.mean(o * o, axis=0, keepdims=True) + SUBLN_EPS)
        o = (o * g_ref[...]) * (1.0 - lam_init)
        o_ref[0, 0, pl.ds(pl.multiple_of(qi * tq, tq), tq), :] = o.T.astype(BF16)
        return carry

    lax.fori_loop(0, nblk, finish, 0)


def _attn(q, k, v, lam_row, subln_g, lam_init):
    bsz, _, seq, _ = q.shape
    tq = BLK
    nblk = seq // tq
    whole = pl.BlockSpec((1, 1, seq, V_HEAD_DIM), lambda b, h: (b, h, 0, 0))
    return pl.pallas_call(
        functools.partial(_attn_kernel, lam_init=lam_init),
        grid=(bsz, N_HEADS),
        in_specs=[whole, whole, whole, _const_spec((tq, 2 * tq)),
                  _const_spec((1, V_HEAD_DIM)), _const_spec((V_HEAD_DIM, 1))],
        out_specs=whole,
        out_shape=jax.ShapeDtypeStruct(q.shape, BF16),
        scratch_shapes=[
            pltpu.VMEM((nblk, V_HEAD_DIM, 2 * tq), BF16),
            pltpu.VMEM((nblk, V_ROWS, tq), BF16),
            pltpu.VMEM((tq, 2 * tq), F32),
            pltpu.VMEM((tq, 2 * tq), F32),
            pltpu.VMEM((nblk, 8, 2 * tq), F32),
            pltpu.VMEM((nblk, V_ROWS, 2 * tq), F32),
        ],
        compiler_params=_cparams(("parallel", "parallel")),
        name="diff_attn",
    )(q, k, v, _causal_bias(), lam_row, subln_g)


def _final_kernel(o_ref, z_ref, h_ref, w_ref, g_ref, out_ref):
    o = jnp.concatenate([o_ref[0, hd] for hd in range(N_HEADS)], axis=1)
    y = o.astype(F32) * jax.nn.silu(z_ref[0].astype(F32))
    h = h_ref[0] + jnp.dot(y.astype(BF16), w_ref[...], preferred_element_type=F32)
    out = h * lax.rsqrt(jnp.mean(h * h, axis=-1, keepdims=True) + NORM_EPS) * g_ref[...]
    for j in range(SSM_L):
        out_ref[0, :, j, :] = out[j * BLK_CHUNKS:(j + 1) * BLK_CHUNKS]


def _final(o, z2, h1, out_w, g):
    bsz, seq, _ = h1.shape
    tok = pl.BlockSpec((1, BLK, D_MODEL), lambda b, i: (b, i, 0))
    return pl.pallas_call(
        _final_kernel,
        grid=(bsz, seq // BLK),
        in_specs=[pl.BlockSpec((1, N_HEADS, BLK, V_HEAD_DIM), lambda b, i: (b, 0, i, 0)), tok, tok,
                  _const_spec((D_MODEL, D_MODEL)), _const_spec((1, D_MODEL))],
        out_specs=_NATURAL_TOK,
        out_shape=jax.ShapeDtypeStruct((bsz, seq // SSM_L, SSM_L, D_MODEL), F32),
        compiler_params=_cparams(("parallel", "parallel")),
        name="attn_out_final",
    )(o, z2, h1, out_w, g).reshape(bsz, seq, D_MODEL)


def _rope_tables(seq):
    half = ROT_DIM // 2
    inv = ROPE_THETA ** (-jnp.arange(0, ROT_DIM, 2, dtype=F32) / ROT_DIM)
    row = jnp.arange(seq, dtype=jnp.int32)
    pos = (row // BLK) * BLK + _block_position(row % BLK)
    ang = pos.astype(F32)[:, None] * inv[None, :]
    cos, sin = jnp.cos(ang), jnp.sin(ang)
    ones = jnp.ones((seq, HEAD_DIM - ROT_DIM), F32)
    zeros = jnp.zeros((seq, HEAD_DIM - ROT_DIM), F32)
    zh = jnp.zeros((seq, half), F32)
    cos_h = jnp.concatenate([cos, cos, ones], axis=1)
    lo_h = jnp.concatenate([-sin, zh, zeros], axis=1)
    hi_h = jnp.concatenate([zh, sin, zeros], axis=1)
    rep = LANES // HEAD_DIM
    return jnp.tile(cos_h, (1, rep)), jnp.tile(lo_h, (1, rep)), jnp.tile(hi_h, (1, rep))


def _pick(n, want):
    r = min(n, want)
    assert n % r == 0, (n, want)
    return r


def kernel(x, a_norm_g, a_in_w, a_lambda_re, a_lambda_im, a_log_dt, a_b_re, a_b_im, a_c_re, a_c_im, a_d, a_glu_w, a_glu_b, a_out_w, kv_norm_g, kv_w, b_norm_g, b_in_w, b_lambda_q1, b_lambda_k1, b_lambda_q2, b_lambda_k2, b_subln_g, b_out_w, final_norm_g):
    bsz, seq, d = x.shape
    assert d == D_MODEL and seq % BLK == 0
    nchunk = seq // SSM_L
    x = x.reshape(bsz, nchunk, SSM_L, D_MODEL)

    u, z = _inproj(x, a_norm_g[0][None], a_in_w[0].astype(BF16))
    a_c, c_c, k_c, a_l = _ssm_tables(a_lambda_re[0], a_lambda_im[0], a_log_dt[0], a_b_re[0],
                                     a_b_im[0], a_c_re[0], a_c_im[0], a_d[0])
    y = _ssm(u, a_c, c_c, k_c, a_l, _pick(nchunk, 512))

    cos_t, slo_t, shi_t = _rope_tables(seq)
    h1, q, k, v, z2 = _mid(y, z, x, cos_t, slo_t, shi_t,
                           a_glu_w[0].astype(BF16), a_glu_b[0][None], a_out_w[0].astype(BF16),
                           kv_norm_g[None], kv_w.astype(BF16),
                           b_norm_g[0][None], b_in_w[0].astype(BF16))

    lam_init = 0.8 - 0.6 * math.exp(-0.3 * N_A_LAYERS)
    lam = (jnp.exp(jnp.sum(b_lambda_q1[0] * b_lambda_k1[0]))
           - jnp.exp(jnp.sum(b_lambda_q2[0] * b_lambda_k2[0])) + lam_init)
    lam_row = jnp.full((1, V_HEAD_DIM), lam, F32)
    o = _attn(q, k, v, lam_row, b_subln_g[0][:, None], lam_init)

    return _final(o, z2, h1, b_out_w[0].astype(BF16), final_norm_g[None])
```

```python
import functools
import math

import jax
import jax.numpy as jnp
import numpy as np
from jax import lax
from jax.experimental import pallas as pl
from jax.experimental.pallas import tpu as pltpu

F32 = jnp.float32
BF16 = jnp.bfloat16

D_MODEL = 1024
SSM_GROUPS = 64
SSM_GROUP = 16
SSM_STATE = 64
N_HEADS = 8
HEAD_DIM = 64
V_HEAD_DIM = 128
ROT_DIM = 16
ROPE_THETA = 500000.0
NORM_EPS = 1e-6
SUBLN_EPS = 1e-5
N_A_LAYERS = 1

LANES = 128
SSM_L = 8
GROUPS_PER_TILE = LANES // SSM_GROUP
N_TILES = D_MODEL // LANES
TILE_STATE = GROUPS_PER_TILE * SSM_STATE
BLK = 512
BLK_CHUNKS = BLK // SSM_L
MASK_VALUE = -1e30
Q_SCALE = HEAD_DIM ** -0.5 * math.log2(math.e)
VMEM_LIMIT = 56 * 1024 * 1024


def _cparams(sem):
    return pltpu.CompilerParams(dimension_semantics=sem, vmem_limit_bytes=VMEM_LIMIT)


def _const_spec(shape):
    return pl.BlockSpec(shape, lambda *_: (0,) * len(shape), pipeline_mode=pl.Buffered(1))


def _load_phase_major(x_ref):
    return jnp.concatenate([x_ref[0, :, j, :] for j in range(SSM_L)], axis=0)


_NATURAL_TOK = pl.BlockSpec((1, BLK_CHUNKS, SSM_L, D_MODEL), lambda b, c: (b, c, 0, 0))


def _block_position(row):
    return (row % BLK_CHUNKS) * SSM_L + row // BLK_CHUNKS


def _inproj_kernel(x_ref, g_ref, w_ref, u_ref, z_ref):
    x = _load_phase_major(x_ref)
    xn = x * lax.rsqrt(jnp.mean(x * x, axis=-1, keepdims=True) + NORM_EPS) * g_ref[...]
    uz = jnp.dot(xn.astype(BF16), w_ref[...], preferred_element_type=F32)
    for j in range(SSM_L):
        rows = slice(j * BLK_CHUNKS, (j + 1) * BLK_CHUNKS)
        for kt in range(N_TILES):
            u_ref[0, kt, j] = uz[rows, kt * LANES:(kt + 1) * LANES].astype(BF16)
    z_ref[0] = uz[:, D_MODEL:].astype(BF16)


def _inproj(x, g, w):
    bsz, nchunk, _, _ = x.shape
    seq = nchunk * SSM_L
    tok = pl.BlockSpec((1, BLK, D_MODEL), lambda b, c: (b, c, 0))
    return pl.pallas_call(
        _inproj_kernel,
        grid=(bsz, seq // BLK),
        in_specs=[_NATURAL_TOK, _const_spec((1, D_MODEL)), _const_spec((D_MODEL, 2 * D_MODEL))],
        out_specs=[
            pl.BlockSpec((1, N_TILES, SSM_L, BLK_CHUNKS, LANES), lambda b, c: (b, 0, 0, c, 0)),
            tok,
        ],
        out_shape=[
            jax.ShapeDtypeStruct((bsz, N_TILES, SSM_L, nchunk, LANES), BF16),
            jax.ShapeDtypeStruct((bsz, seq, D_MODEL), BF16),
        ],
        compiler_params=_cparams(("parallel", "parallel")),
        name="s5_inproj",
    )(x, g, w)


def _ssm_tables(lam_re, lam_im, log_dt, b_re, b_im, c_re, c_im, d_skip):
    hi = lax.Precision.HIGHEST
    n_l = SSM_L
    dt = jnp.exp(log_dt)[:, None]
    lr = lam_re * dt
    li = lam_im * dt
    mag = jnp.exp(lr)
    abar_re = mag * jnp.cos(li)
    abar_im = mag * jnp.sin(li)
    den = lam_re * lam_re + lam_im * lam_im
    nr = abar_re - 1.0
    ni = abar_im
    f_re = (nr * lam_re + ni * lam_im) / den
    f_im = (ni * lam_re - nr * lam_im) / den
    bbar_re = f_re[..., None] * b_re - f_im[..., None] * b_im
    bbar_im = f_re[..., None] * b_im + f_im[..., None] * b_re
    n = jnp.arange(n_l + 1, dtype=F32)[:, None, None]
    pmag = jnp.exp(n * lr[None])
    pw_re = pmag * jnp.cos(n * li[None])
    pw_im = pmag * jnp.sin(n * li[None])
    ab_re = pw_re[..., None] * bbar_re[None] - pw_im[..., None] * bbar_im[None]
    ab_im = pw_re[..., None] * bbar_im[None] + pw_im[..., None] * bbar_re[None]
    kd = (jnp.einsum('gop,tgpi->tgio', c_re, ab_re[:n_l], precision=hi)
          - jnp.einsum('gop,tgpi->tgio', c_im, ab_im[:n_l], precision=hi))
    d_diag = d_skip.reshape(SSM_GROUPS, SSM_GROUP)[:, :, None] * jnp.eye(SSM_GROUP, dtype=F32)
    kd = kd.at[0].add(d_diag)
    ca_re = c_re[None] * pw_re[1:, :, None, :] - c_im[None] * pw_im[1:, :, None, :]
    ca_im = c_re[None] * pw_im[1:, :, None, :] + c_im[None] * pw_re[1:, :, None, :]

    gt, nt = GROUPS_PER_TILE, N_TILES
    n_q = n_l // 2
    ab = jnp.flip(jnp.stack([ab_re[:n_l], ab_im[:n_l]], axis=0), axis=1)
    a_c = ab.reshape(2, n_l, nt, gt, SSM_STATE, SSM_GROUP).transpose(2, 1, 3, 5, 0, 4)
    a_c = a_c.reshape(nt, n_l * LANES, 2 * SSM_STATE)
    ca = jnp.stack([ca_re, -ca_im], axis=0)
    c_c = ca.reshape(2, n_q, 2, nt, gt, SSM_GROUP, SSM_STATE).transpose(3, 0, 4, 6, 1, 2, 5)
    c_c = c_c.reshape(nt, 2 * TILE_STATE, n_l * SSM_GROUP)
    tt = jnp.arange(n_l)[:, None]
    jj = jnp.arange(n_l)[None, :]
    lag = jnp.where(jj <= tt, tt - jj, n_l)
    k_c = jnp.concatenate([kd, jnp.zeros_like(kd[:1])], axis=0)[lag]
    k_c = k_c.reshape(n_q, 2, n_l, nt, gt, SSM_GROUP, SSM_GROUP).transpose(3, 2, 4, 5, 0, 1, 6)
    k_c = k_c.reshape(nt, n_l * LANES, n_l * SSM_GROUP)
    a_l = jnp.concatenate([pw_re[n_l].reshape(nt, 1, TILE_STATE),
                           pw_im[n_l].reshape(nt, 1, TILE_STATE)], axis=-1)
    return a_c.astype(BF16), c_c.astype(BF16), k_c.astype(BF16), a_l


def _selection_matrices():
    r = np.arange(2 * SSM_STATE)[:, None]
    c = np.arange(2 * TILE_STATE)[None, :]
    e_in = (r // SSM_STATE == c // TILE_STATE) & (r % SSM_STATE == c % SSM_STATE)
    r = np.arange(SSM_L * SSM_GROUP)[:, None]
    c = np.arange(2 * LANES)[None, :]
    same = ((r // SSM_GROUP) % 2 == c // LANES) & (r % SSM_GROUP == c % SSM_GROUP)
    e_q = np.stack([same & (r // (2 * SSM_GROUP) == q) for q in range(SSM_L // 2)])
    return jnp.asarray(e_in, BF16), jnp.asarray(e_q, BF16)


def _expand_groups(compact, sel, row_group_width, col_group_width):
    full = jnp.dot(compact, sel, preferred_element_type=F32)
    row = lax.broadcasted_iota(jnp.int32, full.shape, 0)
    col = lax.broadcasted_iota(jnp.int32, full.shape, 1)
    gi = (row >> (row_group_width.bit_length() - 1)) & (GROUPS_PER_TILE - 1)
    gj = (col >> (col_group_width.bit_length() - 1)) & (GROUPS_PER_TILE - 1)
    return jnp.where(gi == gj, full, 0.0).astype(BF16)


def _ssm_kernel(u_ref, ac_ref, cc_ref, kc_ref, ein_ref, eq_ref, al_ref, y_ref,
                su_scr, w_scr, carry_scr, win_scr, wst_scr, wu_scr):
    bsz = u_ref.shape[0]
    rows = u_ref.shape[3]
    ts = TILE_STATE

    @pl.when(pl.program_id(1) == 0)
    def _():
        win_scr[...] = _expand_groups(ac_ref[0], ein_ref[...], SSM_GROUP, SSM_STATE)
        for q in range(SSM_L // 2):
            wst_scr[q] = _expand_groups(cc_ref[0], eq_ref[q], SSM_STATE, SSM_GROUP)
            wu_scr[q] = _expand_groups(kc_ref[0], eq_ref[q], SSM_GROUP, SSM_GROUP)
        carry_scr[...] = jnp.zeros_like(carry_scr)

    for b in range(bsz):
        for j in range(SSM_L):
            su_scr[b * rows:(b + 1) * rows, 2 * ts + j * LANES:2 * ts + (j + 1) * LANES] = u_ref[b, 0, j]
    w_scr[...] = jnp.dot(su_scr[:, 2 * ts:], win_scr[...], preferred_element_type=F32)

    a_re = al_ref[0, :, :ts]
    a_im = al_ref[0, :, ts:]

    def body(c, carry):
        out = []
        for b in range(bsz):
            s_re, s_im = carry[2 * b], carry[2 * b + 1]
            row = pl.ds(b * rows + c, 1)
            w_re = w_scr[row, :ts]
            w_im = w_scr[row, ts:]
            w_scr[row, :ts] = s_re
            w_scr[row, ts:] = s_im
            out += [a_re * s_re - a_im * s_im + w_re, a_re * s_im + a_im * s_re + w_im]
        return tuple(out)

    init = []
    for b in range(bsz):
        init += [carry_scr[8 * b:8 * b + 1, :ts], carry_scr[8 * b:8 * b + 1, ts:]]
    final = lax.fori_loop(0, rows, body, tuple(init))
    for b in range(bsz):
        carry_scr[8 * b:8 * b + 1, :ts] = final[2 * b]
        carry_scr[8 * b:8 * b + 1, ts:] = final[2 * b + 1]

    su_scr[:, :2 * ts] = w_scr[...].astype(BF16)
    for q in range(SSM_L // 2):
        k_used = (2 * q + 2) * LANES
        yy = (jnp.dot(su_scr[:, :2 * ts], wst_scr[q], preferred_element_type=F32)
              + jnp.dot(su_scr[:, 2 * ts:2 * ts + k_used], wu_scr[q, :k_used, :],
                        preferred_element_type=F32))
        for b in range(bsz):
            y_ref[b, 0, 2 * q] = yy[b * rows:(b + 1) * rows, :LANES].astype(BF16)
            y_ref[b, 0, 2 * q + 1] = yy[b * rows:(b + 1) * rows, LANES:].astype(BF16)


def _ssm(u, a_c, c_c, k_c, a_l, rows):
    bsz, _, _, nchunk, _ = u.shape
    k_all = 2 * TILE_STATE + SSM_L * LANES
    blk = pl.BlockSpec((bsz, 1, SSM_L, rows, LANES), lambda t, i: (0, t, 0, i, 0))
    return pl.pallas_call(
        _ssm_kernel,
        grid=(N_TILES, nchunk // rows),
        in_specs=[
            blk,
            pl.BlockSpec((1, SSM_L * LANES, 2 * SSM_STATE), lambda t, i: (t, 0, 0)),
            pl.BlockSpec((1, 2 * TILE_STATE, SSM_L * SSM_GROUP), lambda t, i: (t, 0, 0)),
            pl.BlockSpec((1, SSM_L * LANES, SSM_L * SSM_GROUP), lambda t, i: (t, 0, 0)),
            _const_spec((2 * SSM_STATE, 2 * TILE_STATE)),
            _const_spec((SSM_L // 2, SSM_L * SSM_GROUP, 2 * LANES)),
            pl.BlockSpec((1, 1, 2 * TILE_STATE), lambda t, i: (t, 0, 0)),
        ],
        out_specs=blk,
        out_shape=jax.ShapeDtypeStruct(u.shape, BF16),
        scratch_shapes=[
            pltpu.VMEM((bsz * rows, k_all), BF16),
            pltpu.VMEM((bsz * rows, 2 * TILE_STATE), F32),
            pltpu.VMEM((8 * bsz, 2 * TILE_STATE), F32),
            pltpu.VMEM((SSM_L * LANES, 2 * TILE_STATE), BF16),
            pltpu.VMEM((SSM_L // 2, 2 * TILE_STATE, 2 * LANES), BF16),
            pltpu.VMEM((SSM_L // 2, SSM_L * LANES, 2 * LANES), BF16),
        ],
        compiler_params=_cparams(("parallel", "arbitrary")),
        name="s5_scan",
    )(u, a_c, c_c, k_c, *_selection_matrices(), a_l)


def _rope(t, cos_f, sin_lo, sin_hi):
    half = ROT_DIM // 2
    width = t.shape[1]
    return (t * cos_f + pltpu.roll(t, half, 1) * sin_hi
            + pltpu.roll(t, width - half, 1) * sin_lo)


def _mid_kernel(y_ref, z_ref, x_ref, cos_ref, slo_ref, shi_ref, gluw_ref, glub_ref, outw_ref,
                kvg_ref, kvw_ref, bg_ref, binw_ref,
                h_ref, q_ref, k_ref, v_ref, z2_ref):
    y = jnp.concatenate(
        [jnp.concatenate([y_ref[0, kt, j] for kt in range(N_TILES)], axis=1) for j in range(SSM_L)],
        axis=0).astype(F32)
    y = 0.5 * y * (1.0 + lax.erf(y * (2.0 ** -0.5)))
    gate = jnp.dot(y.astype(BF16), gluw_ref[...], preferred_element_type=F32) + glub_ref[...]
    y = y * jax.nn.sigmoid(gate)
    y = y * jax.nn.silu(z_ref[0].astype(F32))
    h = _load_phase_major(x_ref) + jnp.dot(y.astype(BF16), outw_ref[...], preferred_element_type=F32)
    h_ref[0] = h

    hn = h * lax.rsqrt(jnp.mean(h * h, axis=-1, keepdims=True) + NORM_EPS)
    reps = D_MODEL // LANES
    cos_f = jnp.concatenate([cos_ref[...]] * reps, axis=1)
    sin_lo = jnp.concatenate([slo_ref[...]] * reps, axis=1)
    sin_hi = jnp.concatenate([shi_ref[...]] * reps, axis=1)

    kv = jnp.dot((hn * kvg_ref[...]).astype(BF16), kvw_ref[...], preferred_element_type=F32)
    k = _rope(kv[:, :D_MODEL], cos_f, sin_lo, sin_hi).astype(BF16)
    v = kv[:, D_MODEL:].astype(BF16)
    qz = jnp.dot((hn * bg_ref[...]).astype(BF16), binw_ref[...], preferred_element_type=F32)
    q = (_rope(qz[:, :D_MODEL], cos_f, sin_lo, sin_hi) * Q_SCALE).astype(BF16)
    for hd in range(N_HEADS):
        cols = slice(hd * V_HEAD_DIM, (hd + 1) * V_HEAD_DIM)
        q_ref[0, hd] = q[:, cols]
        k_ref[0, hd] = k[:, cols]
        v_ref[0, hd] = v[:, cols]
    z2_ref[0] = qz[:, D_MODEL:].astype(BF16)


def _mid(y, z, x, cos_t, slo_t, shi_t, glu_w, glu_b, out_w, kv_g, kv_w, b_g, bin_w):
    bsz, nchunk, _, _ = x.shape
    seq = nchunk * SSM_L
    tok = pl.BlockSpec((1, BLK, D_MODEL), lambda b, c: (b, c, 0))
    tab = pl.BlockSpec((BLK, LANES), lambda b, c: (c, 0))
    heads = pl.BlockSpec((1, N_HEADS, BLK, V_HEAD_DIM), lambda b, c: (b, 0, c, 0))
    per_head = jax.ShapeDtypeStruct((bsz, N_HEADS, seq, V_HEAD_DIM), BF16)
    return pl.pallas_call(
        _mid_kernel,
        grid=(bsz, seq // BLK),
        in_specs=[
            pl.BlockSpec((1, N_TILES, SSM_L, BLK_CHUNKS, LANES), lambda b, c: (b, 0, 0, c, 0)),
            tok, _NATURAL_TOK, tab, tab, tab,
            _const_spec((D_MODEL, D_MODEL)), _const_spec((1, D_MODEL)), _const_spec((D_MODEL, D_MODEL)),
            _const_spec((1, D_MODEL)), _const_spec((D_MODEL, 2 * D_MODEL)),
            _const_spec((1, D_MODEL)), _const_spec((D_MODEL, 2 * D_MODEL)),
        ],
        out_specs=[tok, heads, heads, heads, tok],
        out_shape=[jax.ShapeDtypeStruct((bsz, seq, D_MODEL), F32), per_head, per_head, per_head,
                   jax.ShapeDtypeStruct((bsz, seq, D_MODEL), BF16)],
        compiler_params=_cparams(("parallel", "parallel")),
        name="s5_out_qkv",
    )(y, z, x, cos_t, slo_t, shi_t, glu_w, glu_b, out_w, kv_g, kv_w, b_g, bin_w)


V_ROWS = V_HEAD_DIM + 16
WALK_UNROLL = 8


def _causal_bias():
    pos = _block_position(np.arange(BLK))
    visible = pos[:, None] <= np.concatenate([pos, pos])[None, :]
    return jnp.asarray(np.where(visible, 0.0, MASK_VALUE), F32)


def _attn_kernel(q_ref, k_ref, v_ref, bias_ref, lam_ref, g_ref, o_ref,
                 qs_scr, vt_scr, sa_scr, sb_scr, m_scr, acc_scr, *, lam_init):
    tq = BLK
    nblk = k_ref.shape[2] // tq

    for c in range(nblk):
        rows = slice(c * tq, (c + 1) * tq)
        vt_scr[c, :V_HEAD_DIM, :] = v_ref[0, 0, rows, :].astype(F32).T.astype(BF16)
        one_row = lax.broadcasted_iota(jnp.int32, (V_ROWS - V_HEAD_DIM, tq), 0) == 0
        vt_scr[c, V_HEAD_DIM:, :] = jnp.where(one_row, 1.0, 0.0).astype(BF16)
        qt = q_ref[0, 0, rows, :].astype(F32).T
        dim = lax.broadcasted_iota(jnp.int32, qt.shape, 0)
        qs_scr[c, :, :tq] = jnp.where(dim < HEAD_DIM, qt, 0.0).astype(BF16)
        qs_scr[c, :, tq:] = jnp.where(dim >= HEAD_DIM, qt, 0.0).astype(BF16)
    m_scr[...] = jnp.full_like(m_scr, MASK_VALUE)
    acc_scr[...] = jnp.zeros_like(acc_scr)

    def scores(qi, kb, s_scr, diagonal):
        start = pl.multiple_of(kb * tq, tq)
        k = k_ref[0, 0, pl.ds(start, tq), :]
        s = jnp.dot(k, qs_scr[qi], preferred_element_type=F32)
        s_scr[...] = s + bias_ref[...] if diagonal else s

    def update(qi, kb, s_scr):
        s = s_scr[...]
        m_prev = m_scr[qi, 0:1, :]
        m_new = jnp.maximum(m_prev, jnp.max(s, axis=0, keepdims=True))
        alpha = jnp.exp2(m_prev - m_new)
        p = jnp.exp2(s - m_new).astype(BF16)
        pv = jnp.dot(vt_scr[kb], p, preferred_element_type=F32)
        acc_scr[qi] = alpha * acc_scr[qi] + pv
        m_scr[qi, 0:1, :] = m_new

    def walk(count, first, following, diagonal):
        if count == 0:
            return
        last_blk = nblk - 1
        scores(first[0], first[1], sa_scr, diagonal)

        def steps(n):
            def body(_, cur):
                for step in range(n):
                    here, there = (sa_scr, sb_scr) if step % 2 == 0 else (sb_scr, sa_scr)
                    nxt = following(*cur)
                    scores(jnp.minimum(nxt[0], last_blk), jnp.minimum(nxt[1], last_blk), there, diagonal)
                    update(cur[0], cur[1], here)
                    cur = nxt
                return cur
            return body

        cur = (jnp.int32(first[0]), jnp.int32(first[1]))
        cur = lax.fori_loop(0, count // WALK_UNROLL, steps(WALK_UNROLL), cur)
        for _ in range(count % WALK_UNROLL // 2):
            cur = steps(2)(0, cur)
        if count % 2:
            update(cur[0], cur[1], sa_scr)

    def below_next(qi, kb):
        last = kb == qi - 1
        return jnp.where(last, qi + 1, qi), jnp.where(last, 0, kb + 1)

    walk(nblk * (nblk - 1) // 2, (1, 0), below_next, False)
    walk(nblk, (0, 0), lambda qi, kb: (qi + 1, kb + 1), True)

    def finish(qi, carry):
        acc = acc_scr[qi]
        o_all = acc[:V_HEAD_DIM] / acc[V_HEAD_DIM:V_HEAD_DIM + 1]
        o = o_all[:, :tq] - lam_ref[:, 0:1] * o_all[:, tq:]
        o = o * lax.rsqrt(jnp.mean(o * o, axis=0, keepdims=True) + SUBLN_EPS)
        o = (o * g_ref[...]) * (1.0 - lam_init)
        o_ref[0, 0, pl.ds(pl.multiple_of(qi * tq, tq), tq), :] = o.T.astype(BF16)
        return carry

    lax.fori_loop(0, nblk, finish, 0)


def _attn(q, k, v, lam_row, subln_g, lam_init):
    bsz, _, seq, _ = q.shape
    tq = BLK
    nblk = seq // tq
    whole = pl.BlockSpec((1, 1, seq, V_HEAD_DIM), lambda b, h: (b, h, 0, 0))
    return pl.pallas_call(
        functools.partial(_attn_kernel, lam_init=lam_init),
        grid=(bsz, N_HEADS),
        in_specs=[whole, whole, whole, _const_spec((tq, 2 * tq)),
                  _const_spec((1, V_HEAD_DIM)), _const_spec((V_HEAD_DIM, 1))],
        out_specs=whole,
        out_shape=jax.ShapeDtypeStruct(q.shape, BF16),
        scratch_shapes=[
            pltpu.VMEM((nblk, V_HEAD_DIM, 2 * tq), BF16),
            pltpu.VMEM((nblk, V_ROWS, tq), BF16),
            pltpu.VMEM((tq, 2 * tq), F32),
            pltpu.VMEM((tq, 2 * tq), F32),
            pltpu.VMEM((nblk, 8, 2 * tq), F32),
            pltpu.VMEM((nblk, V_ROWS, 2 * tq), F32),
        ],
        compiler_params=_cparams(("parallel", "parallel")),
        name="diff_attn",
    )(q, k, v, _causal_bias(), lam_row, subln_g)


def _final_kernel(o_ref, z_ref, h_ref, w_ref, g_ref, out_ref):
    o = jnp.concatenate([o_ref[0, hd] for hd in range(N_HEADS)], axis=1)
    y = o.astype(F32) * jax.nn.silu(z_ref[0].astype(F32))
    h = h_ref[0] + jnp.dot(y.astype(BF16), w_ref[...], preferred_element_type=F32)
    out = h * lax.rsqrt(jnp.mean(h * h, axis=-1, keepdims=True) + NORM_EPS) * g_ref[...]
    for j in range(SSM_L):
        out_ref[0, :, j, :] = out[j * BLK_CHUNKS:(j + 1) * BLK_CHUNKS]


def _final(o, z2, h1, out_w, g):
    bsz, seq, _ = h1.shape
    tok = pl.BlockSpec((1, BLK, D_MODEL), lambda b, i: (b, i, 0))
    return pl.pallas_call(
        _final_kernel,
        grid=(bsz, seq // BLK),
        in_specs=[pl.BlockSpec((1, N_HEADS, BLK, V_HEAD_DIM), lambda b, i: (b, 0, i, 0)), tok, tok,
                  _const_spec((D_MODEL, D_MODEL)), _const_spec((1, D_MODEL))],
        out_specs=_NATURAL_TOK,
        out_shape=jax.ShapeDtypeStruct((bsz, seq // SSM_L, SSM_L, D_MODEL), F32),
        compiler_params=_cparams(("parallel", "parallel")),
        name="attn_out_final",
    )(o, z2, h1, out_w, g).reshape(bsz, seq, D_MODEL)


def _rope_tables(seq):
    half = ROT_DIM // 2
    inv = ROPE_THETA ** (-jnp.arange(0, ROT_DIM, 2, dtype=F32) / ROT_DIM)
    row = jnp.arange(seq, dtype=jnp.int32)
    pos = (row // BLK) * BLK + _block_position(row % BLK)
    ang = pos.astype(F32)[:, None] * inv[None, :]
    cos, sin = jnp.cos(ang), jnp.sin(ang)
    ones = jnp.ones((seq, HEAD_DIM - ROT_DIM), F32)
    zeros = jnp.zeros((seq, HEAD_DIM - ROT_DIM), F32)
    zh = jnp.zeros((seq, half), F32)
    cos_h = jnp.concatenate([cos, cos, ones], axis=1)
    lo_h = jnp.concatenate([-sin, zh, zeros], axis=1)
    hi_h = jnp.concatenate([zh, sin, zeros], axis=1)
    rep = LANES // HEAD_DIM
    return jnp.tile(cos_h, (1, rep)), jnp.tile(lo_h, (1, rep)), jnp.tile(hi_h, (1, rep))


def _pick(n, want):
    r = min(n, want)
    assert n % r == 0, (n, want)
    return r


def kernel(x, a_norm_g, a_in_w, a_lambda_re, a_lambda_im, a_log_dt, a_b_re, a_b_im, a_c_re, a_c_im, a_d, a_glu_w, a_glu_b, a_out_w, kv_norm_g, kv_w, b_norm_g, b_in_w, b_lambda_q1, b_lambda_k1, b_lambda_q2, b_lambda_k2, b_subln_g, b_out_w, final_norm_g):
    bsz, seq, d = x.shape
    assert d == D_MODEL and seq % BLK == 0
    nchunk = seq // SSM_L
    x = x.reshape(bsz, nchunk, SSM_L, D_MODEL)

    u, z = _inproj(x, a_norm_g[0][None], a_in_w[0].astype(BF16))
    a_c, c_c, k_c, a_l = _ssm_tables(a_lambda_re[0], a_lambda_im[0], a_log_dt[0], a_b_re[0],
                                     a_b_im[0], a_c_re[0], a_c_im[0], a_d[0])
    y = _ssm(u, a_c, c_c, k_c, a_l, _pick(nchunk, 256))

    cos_t, slo_t, shi_t = _rope_tables(seq)
    h1, q, k, v, z2 = _mid(y, z, x, cos_t, slo_t, shi_t,
                           a_glu_w[0].astype(BF16), a_glu_b[0][None], a_out_w[0].astype(BF16),
                           kv_norm_g[None], kv_w.astype(BF16),
                           b_norm_g[0][None], b_in_w[0].astype(BF16))

    lam_init = 0.8 - 0.6 * math.exp(-0.3 * N_A_LAYERS)
    lam = (jnp.exp(jnp.sum(b_lambda_q1[0] * b_lambda_k1[0]))
           - jnp.exp(jnp.sum(b_lambda_q2[0] * b_lambda_k2[0])) + lam_init)
    lam_row = jnp.full((1, V_HEAD_DIM), lam, F32)
    o = _attn(q, k, v, lam_row, b_subln_g[0][:, None], lam_init)

    return _final(o, z2, h1, b_out_w[0].astype(BF16), final_norm_g[None])
```

```python
import functools
import math

import jax
import jax.numpy as jnp
import numpy as np
from jax import lax
from jax.experimental import pallas as pl
from jax.experimental.pallas import tpu as pltpu

F32 = jnp.float32
BF16 = jnp.bfloat16

D_MODEL = 1024
SSM_GROUPS = 64
SSM_GROUP = 16
SSM_STATE = 64
N_HEADS = 8
HEAD_DIM = 64
V_HEAD_DIM = 128
ROT_DIM = 16
ROPE_THETA = 500000.0
NORM_EPS = 1e-6
SUBLN_EPS = 1e-5
N_A_LAYERS = 1

LANES = 128
SSM_L = 8
GROUPS_PER_TILE = LANES // SSM_GROUP
N_TILES = D_MODEL // LANES
TILE_STATE = GROUPS_PER_TILE * SSM_STATE
BLK = 512
BLK_CHUNKS = BLK // SSM_L
MASK_VALUE = -1e30
Q_SCALE = HEAD_DIM ** -0.5 * math.log2(math.e)
VMEM_LIMIT = 56 * 1024 * 1024


def _cparams(sem):
    return pltpu.CompilerParams(dimension_semantics=sem, vmem_limit_bytes=VMEM_LIMIT)


def _const_spec(shape):
    return pl.BlockSpec(shape, lambda *_: (0,) * len(shape), pipeline_mode=pl.Buffered(1))


def _load_phase_major(x_ref):
    return jnp.concatenate([x_ref[0, :, j, :] for j in range(SSM_L)], axis=0)


_NATURAL_TOK = pl.BlockSpec((1, BLK_CHUNKS, SSM_L, D_MODEL), lambda b, c: (b, c, 0, 0))


def _block_position(row):
    return (row % BLK_CHUNKS) * SSM_L + row // BLK_CHUNKS


def _inproj_kernel(x_ref, g_ref, w_ref, u_ref, z_ref):
    x = _load_phase_major(x_ref)
    xn = x * lax.rsqrt(jnp.mean(x * x, axis=-1, keepdims=True) + NORM_EPS) * g_ref[...]
    uz = jnp.dot(xn.astype(BF16), w_ref[...], preferred_element_type=F32)
    for j in range(SSM_L):
        rows = slice(j * BLK_CHUNKS, (j + 1) * BLK_CHUNKS)
        for kt in range(N_TILES):
            u_ref[0, kt, j] = uz[rows, kt * LANES:(kt + 1) * LANES].astype(BF16)
    z_ref[0] = uz[:, D_MODEL:].astype(BF16)


def _inproj(x, g, w):
    bsz, nchunk, _, _ = x.shape
    seq = nchunk * SSM_L
    tok = pl.BlockSpec((1, BLK, D_MODEL), lambda b, c: (b, c, 0))
    return pl.pallas_call(
        _inproj_kernel,
        grid=(bsz, seq // BLK),
        in_specs=[_NATURAL_TOK, _const_spec((1, D_MODEL)), _const_spec((D_MODEL, 2 * D_MODEL))],
        out_specs=[
            pl.BlockSpec((1, N_TILES, SSM_L, BLK_CHUNKS, LANES), lambda b, c: (b, 0, 0, c, 0)),
            tok,
        ],
        out_shape=[
            jax.ShapeDtypeStruct((bsz, N_TILES, SSM_L, nchunk, LANES), BF16),
            jax.ShapeDtypeStruct((bsz, seq, D_MODEL), BF16),
        ],
        compiler_params=_cparams(("parallel", "parallel")),
        name="s5_inproj",
    )(x, g, w)


def _ssm_tables(lam_re, lam_im, log_dt, b_re, b_im, c_re, c_im, d_skip):
    hi = lax.Precision.HIGHEST
    n_l = SSM_L
    dt = jnp.exp(log_dt)[:, None]
    lr = lam_re * dt
    li = lam_im * dt
    mag = jnp.exp(lr)
    abar_re = mag * jnp.cos(li)
    abar_im = mag * jnp.sin(li)
    den = lam_re * lam_re + lam_im * lam_im
    nr = abar_re - 1.0
    ni = abar_im
    f_re = (nr * lam_re + ni * lam_im) / den
    f_im = (ni * lam_re - nr * lam_im) / den
    bbar_re = f_re[..., None] * b_re - f_im[..., None] * b_im
    bbar_im = f_re[..., None] * b_im + f_im[..., None] * b_re
    n = jnp.arange(n_l + 1, dtype=F32)[:, None, None]
    pmag = jnp.exp(n * lr[None])
    pw_re = pmag * jnp.cos(n * li[None])
    pw_im = pmag * jnp.sin(n * li[None])
    gt, nt = GROUPS_PER_TILE, N_TILES
    bt_re = bbar_re.transpose(0, 2, 1)
    bt_im = bbar_im.transpose(0, 2, 1)
    ab_re = pw_re[:n_l, :, None, :] * bt_re[None] - pw_im[:n_l, :, None, :] * bt_im[None]
    ab_im = pw_re[:n_l, :, None, :] * bt_im[None] + pw_im[:n_l, :, None, :] * bt_re[None]
    kd = (jnp.einsum('gop,tgip->tgio', c_re, ab_re, precision=hi)
          - jnp.einsum('gop,tgip->tgio', c_im, ab_im, precision=hi))
    d_diag = d_skip.reshape(SSM_GROUPS, SSM_GROUP)[:, :, None] * jnp.eye(SSM_GROUP, dtype=F32)
    kd = kd.at[0].add(d_diag)

    def per_tile(a):
        lead, _, r, w = a.shape
        return a.reshape(lead, nt, gt * r, w).transpose(1, 0, 2, 3).reshape(nt, lead * gt * r, w)

    a_c = per_tile(jnp.concatenate([jnp.flip(ab_re, 0), jnp.flip(ab_im, 0)], axis=-1))
    cr = jnp.tile(c_re.transpose(0, 2, 1), (1, 1, n_l))
    ci = jnp.tile(c_im.transpose(0, 2, 1), (1, 1, n_l))
    pr = jnp.repeat(pw_re[1:].transpose(1, 2, 0), SSM_GROUP, axis=-1)
    pi = jnp.repeat(pw_im[1:].transpose(1, 2, 0), SSM_GROUP, axis=-1)
    c_c = per_tile(jnp.stack([cr * pr - ci * pi, -(cr * pi + ci * pr)], axis=0))
    zero = jnp.zeros_like(kd[0])
    k_c = per_tile(jnp.stack(
        [jnp.concatenate([kd[t - j] if j <= t else zero for t in range(n_l)], axis=-1) for j in range(n_l)],
        axis=0))
    a_l = jnp.concatenate([pw_re[n_l].reshape(nt, 1, TILE_STATE),
                           pw_im[n_l].reshape(nt, 1, TILE_STATE)], axis=-1)
    return a_c.astype(BF16), c_c.astype(BF16), k_c.astype(BF16), a_l


def _selection_matrices():
    r = np.arange(2 * SSM_STATE)[:, None]
    c = np.arange(2 * TILE_STATE)[None, :]
    e_in = (r // SSM_STATE == c // TILE_STATE) & (r % SSM_STATE == c % SSM_STATE)
    r = np.arange(SSM_L * SSM_GROUP)[:, None]
    c = np.arange(2 * LANES)[None, :]
    same = ((r // SSM_GROUP) % 2 == c // LANES) & (r % SSM_GROUP == c % SSM_GROUP)
    e_q = np.stack([same & (r // (2 * SSM_GROUP) == q) for q in range(SSM_L // 2)])
    return jnp.asarray(e_in, BF16), jnp.asarray(e_q, BF16)


def _expand_groups(compact, sel, row_group_width, col_group_width):
    full = jnp.dot(compact, sel, preferred_element_type=F32)
    row = lax.broadcasted_iota(jnp.int32, full.shape, 0)
    col = lax.broadcasted_iota(jnp.int32, full.shape, 1)
    gi = (row >> (row_group_width.bit_length() - 1)) & (GROUPS_PER_TILE - 1)
    gj = (col >> (col_group_width.bit_length() - 1)) & (GROUPS_PER_TILE - 1)
    return jnp.where(gi == gj, full, 0.0).astype(BF16)


def _ssm_kernel(u_ref, ac_ref, cc_ref, kc_ref, ein_ref, eq_ref, al_ref, y_ref,
                su_scr, w_scr, carry_scr, win_scr, wst_scr, wu_scr):
    bsz = u_ref.shape[0]
    rows = u_ref.shape[3]
    ts = TILE_STATE

    @pl.when(pl.program_id(1) == 0)
    def _():
        win_scr[...] = _expand_groups(ac_ref[0], ein_ref[...], SSM_GROUP, SSM_STATE)
        for q in range(SSM_L // 2):
            wst_scr[q] = _expand_groups(cc_ref[0], eq_ref[q], SSM_STATE, SSM_GROUP)
            wu_scr[q] = _expand_groups(kc_ref[0], eq_ref[q], SSM_GROUP, SSM_GROUP)
        carry_scr[...] = jnp.zeros_like(carry_scr)

    for b in range(bsz):
        for j in range(SSM_L):
            su_scr[b * rows:(b + 1) * rows, 2 * ts + j * LANES:2 * ts + (j + 1) * LANES] = u_ref[b, 0, j]
    w_scr[...] = jnp.dot(su_scr[:, 2 * ts:], win_scr[...], preferred_element_type=F32)

    a_re = al_ref[0, :, :ts]
    a_im = al_ref[0, :, ts:]

    def body(c, carry):
        out = []
        for b in range(bsz):
            s_re, s_im = carry[2 * b], carry[2 * b + 1]
            row = pl.ds(b * rows + c, 1)
            w_re = w_scr[row, :ts]
            w_im = w_scr[row, ts:]
            w_scr[row, :ts] = s_re
            w_scr[row, ts:] = s_im
            out += [a_re * s_re - a_im * s_im + w_re, a_re * s_im + a_im * s_re + w_im]
        return tuple(out)

    init = []
    for b in range(bsz):
        init += [carry_scr[8 * b:8 * b + 1, :ts], carry_scr[8 * b:8 * b + 1, ts:]]
    final = lax.fori_loop(0, rows, body, tuple(init))
    for b in range(bsz):
        carry_scr[8 * b:8 * b + 1, :ts] = final[2 * b]
        carry_scr[8 * b:8 * b + 1, ts:] = final[2 * b + 1]

    su_scr[:, :2 * ts] = w_scr[...].astype(BF16)
    for q in range(SSM_L // 2):
        k_used = (2 * q + 2) * LANES
        yy = (jnp.dot(su_scr[:, :2 * ts], wst_scr[q], preferred_element_type=F32)
              + jnp.dot(su_scr[:, 2 * ts:2 * ts + k_used], wu_scr[q, :k_used, :],
                        preferred_element_type=F32))
        for b in range(bsz):
            y_ref[b, 0, 2 * q] = yy[b * rows:(b + 1) * rows, :LANES].astype(BF16)
            y_ref[b, 0, 2 * q + 1] = yy[b * rows:(b + 1) * rows, LANES:].astype(BF16)


def _ssm(u, a_c, c_c, k_c, a_l, rows):
    bsz, _, _, nchunk, _ = u.shape
    k_all = 2 * TILE_STATE + SSM_L * LANES
    blk = pl.BlockSpec((bsz, 1, SSM_L, rows, LANES), lambda t, i: (0, t, 0, i, 0))
    return pl.pallas_call(
        _ssm_kernel,
        grid=(N_TILES, nchunk // rows),
        in_specs=[
            blk,
            pl.BlockSpec((1, SSM_L * LANES, 2 * SSM_STATE), lambda t, i: (t, 0, 0)),
            pl.BlockSpec((1, 2 * TILE_STATE, SSM_L * SSM_GROUP), lambda t, i: (t, 0, 0)),
            pl.BlockSpec((1, SSM_L * LANES, SSM_L * SSM_GROUP), lambda t, i: (t, 0, 0)),
            _const_spec((2 * SSM_STATE, 2 * TILE_STATE)),
            _const_spec((SSM_L // 2, SSM_L * SSM_GROUP, 2 * LANES)),
            pl.BlockSpec((1, 1, 2 * TILE_STATE), lambda t, i: (t, 0, 0)),
        ],
        out_specs=blk,
        out_shape=jax.ShapeDtypeStruct(u.shape, BF16),
        scratch_shapes=[
            pltpu.VMEM((bsz * rows, k_all), BF16),
            pltpu.VMEM((bsz * rows, 2 * TILE_STATE), F32),
            pltpu.VMEM((8 * bsz, 2 * TILE_STATE), F32),
            pltpu.VMEM((SSM_L * LANES, 2 * TILE_STATE), BF16),
            pltpu.VMEM((SSM_L // 2, 2 * TILE_STATE, 2 * LANES), BF16),
            pltpu.VMEM((SSM_L // 2, SSM_L * LANES, 2 * LANES), BF16),
        ],
        compiler_params=_cparams(("parallel", "arbitrary")),
        name="s5_scan",
    )(u, a_c, c_c, k_c, *_selection_matrices(), a_l)


def _rope(t, cos_f, sin_lo, sin_hi):
    half = ROT_DIM // 2
    width = t.shape[1]
    return (t * cos_f + pltpu.roll(t, half, 1) * sin_hi
            + pltpu.roll(t, width - half, 1) * sin_lo)


def _mid_kernel(y_ref, z_ref, x_ref, cos_ref, slo_ref, shi_ref, gluw_ref, glub_ref, outw_ref,
                kvg_ref, kvw_ref, bg_ref, binw_ref,
                h_ref, q_ref, k_ref, v_ref, z2_ref):
    y = jnp.concatenate(
        [jnp.concatenate([y_ref[0, kt, j] for kt in range(N_TILES)], axis=1) for j in range(SSM_L)],
        axis=0).astype(F32)
    y = 0.5 * y * (1.0 + lax.erf(y * (2.0 ** -0.5)))
    gate = jnp.dot(y.astype(BF16), gluw_ref[...], preferred_element_type=F32) + glub_ref[...]
    y = y * jax.nn.sigmoid(gate)
    y = y * jax.nn.silu(z_ref[0].astype(F32))
    h = _load_phase_major(x_ref) + jnp.dot(y.astype(BF16), outw_ref[...], preferred_element_type=F32)
    h_ref[0] = h

    hn = h * lax.rsqrt(jnp.mean(h * h, axis=-1, keepdims=True) + NORM_EPS)
    reps = D_MODEL // LANES
    cos_f = jnp.concatenate([cos_ref[...]] * reps, axis=1)
    sin_lo = jnp.concatenate([slo_ref[...]] * reps, axis=1)
    sin_hi = jnp.concatenate([shi_ref[...]] * reps, axis=1)

    kv = jnp.dot((hn * kvg_ref[...]).astype(BF16), kvw_ref[...], preferred_element_type=F32)
    k = _rope(kv[:, :D_MODEL], cos_f, sin_lo, sin_hi).astype(BF16)
    v = kv[:, D_MODEL:].astype(BF16)
    qz = jnp.dot((hn * bg_ref[...]).astype(BF16), binw_ref[...], preferred_element_type=F32)
    q = (_rope(qz[:, :D_MODEL], cos_f, sin_lo, sin_hi) * Q_SCALE).astype(BF16)
    z2 = qz[:, D_MODEL:].astype(BF16)
    for hd in range(N_HEADS):
        cols = slice(hd * V_HEAD_DIM, (hd + 1) * V_HEAD_DIM)
        q_ref[0, hd] = q[:, cols]
        k_ref[0, hd] = k[:, cols]
        v_ref[0, hd] = v[:, cols]
        z2_ref[0, hd] = z2[:, cols]


def _mid(y, z, x, cos_t, slo_t, shi_t, glu_w, glu_b, out_w, kv_g, kv_w, b_g, bin_w):
    bsz, nchunk, _, _ = x.shape
    seq = nchunk * SSM_L
    tok = pl.BlockSpec((1, BLK, D_MODEL), lambda b, c: (b, c, 0))
    tab = pl.BlockSpec((BLK, LANES), lambda b, c: (c, 0))
    heads = pl.BlockSpec((1, N_HEADS, BLK, V_HEAD_DIM), lambda b, c: (b, 0, c, 0))
    per_head = jax.ShapeDtypeStruct((bsz, N_HEADS, seq, V_HEAD_DIM), BF16)
    return pl.pallas_call(
        _mid_kernel,
        grid=(bsz, seq // BLK),
        in_specs=[
            pl.BlockSpec((1, N_TILES, SSM_L, BLK_CHUNKS, LANES), lambda b, c: (b, 0, 0, c, 0)),
            tok, _NATURAL_TOK, tab, tab, tab,
            _const_spec((D_MODEL, D_MODEL)), _const_spec((1, D_MODEL)), _const_spec((D_MODEL, D_MODEL)),
            _const_spec((1, D_MODEL)), _const_spec((D_MODEL, 2 * D_MODEL)),
            _const_spec((1, D_MODEL)), _const_spec((D_MODEL, 2 * D_MODEL)),
        ],
        out_specs=[tok, heads, heads, heads, heads],
        out_shape=[jax.ShapeDtypeStruct((bsz, seq, D_MODEL), F32), per_head, per_head, per_head, per_head],
        compiler_params=_cparams(("parallel", "parallel")),
        name="s5_out_qkv",
    )(y, z, x, cos_t, slo_t, shi_t, glu_w, glu_b, out_w, kv_g, kv_w, b_g, bin_w)


V_ROWS = V_HEAD_DIM + 16
WALK_UNROLL = 8


def _causal_bias():
    pos = _block_position(np.arange(BLK))
    visible = pos[:, None] <= np.concatenate([pos, pos])[None, :]
    return jnp.asarray(np.where(visible, 0.0, MASK_VALUE), F32)


def _attn_kernel(q_ref, k_ref, v_ref, z_ref, bias_ref, lam_ref, g_ref, o_ref,
                 qs_scr, vt_scr, sa_scr, sb_scr, m_scr, acc_scr, *, lam_init):
    tq = BLK
    nblk = k_ref.shape[2] // tq

    for c in range(nblk):
        rows = slice(c * tq, (c + 1) * tq)
        vt_scr[c, :V_HEAD_DIM, :] = v_ref[0, 0, rows, :].astype(F32).T.astype(BF16)
        one_row = lax.broadcasted_iota(jnp.int32, (V_ROWS - V_HEAD_DIM, tq), 0) == 0
        vt_scr[c, V_HEAD_DIM:, :] = jnp.where(one_row, 1.0, 0.0).astype(BF16)
        qt = q_ref[0, 0, rows, :].astype(F32).T
        dim = lax.broadcasted_iota(jnp.int32, qt.shape, 0)
        qs_scr[c, :, :tq] = jnp.where(dim < HEAD_DIM, qt, 0.0).astype(BF16)
        qs_scr[c, :, tq:] = jnp.where(dim >= HEAD_DIM, qt, 0.0).astype(BF16)
    m_scr[...] = jnp.full_like(m_scr, MASK_VALUE)
    acc_scr[...] = jnp.zeros_like(acc_scr)

    def scores(qi, kb, s_scr, diagonal):
        start = pl.multiple_of(kb * tq, tq)
        k = k_ref[0, 0, pl.ds(start, tq), :]
        s = jnp.dot(k, qs_scr[qi], preferred_element_type=F32)
        s_scr[...] = s + bias_ref[...] if diagonal else s

    def update(qi, kb, s_scr):
        s = s_scr[...]
        m_prev = m_scr[qi, 0:1, :]
        m_new = jnp.maximum(m_prev, jnp.max(s, axis=0, keepdims=True))
        alpha = jnp.exp2(m_prev - m_new)
        p = jnp.exp2(s - m_new).astype(BF16)
        pv = jnp.dot(vt_scr[kb], p, preferred_element_type=F32)
        acc_scr[qi] = alpha * acc_scr[qi] + pv
        m_scr[qi, 0:1, :] = m_new

    def walk(count, first, following, diagonal):
        if count == 0:
            return
        last_blk = nblk - 1
        scores(first[0], first[1], sa_scr, diagonal)

        def steps(n):
            def body(_, cur):
                for step in range(n):
                    here, there = (sa_scr, sb_scr) if step % 2 == 0 else (sb_scr, sa_scr)
                    nxt = following(*cur)
                    scores(jnp.minimum(nxt[0], last_blk), jnp.minimum(nxt[1], last_blk), there, diagonal)
                    update(cur[0], cur[1], here)
                    cur = nxt
                return cur
            return body

        cur = (jnp.int32(first[0]), jnp.int32(first[1]))
        cur = lax.fori_loop(0, count // WALK_UNROLL, steps(WALK_UNROLL), cur)
        for _ in range(count % WALK_UNROLL // 2):
            cur = steps(2)(0, cur)
        if count % 2:
            update(cur[0], cur[1], sa_scr)

    def below_next(qi, kb):
        last = kb == qi - 1
        return jnp.where(last, qi + 1, qi), jnp.where(last, 0, kb + 1)

    walk(nblk * (nblk - 1) // 2, (1, 0), below_next, False)
    walk(nblk, (0, 0), lambda qi, kb: (qi + 1, kb + 1), True)

    def finish(qi, carry):
        acc = acc_scr[qi]
        o_all = acc[:V_HEAD_DIM] / acc[V_HEAD_DIM:V_HEAD_DIM + 1]
        o = o_all[:, :tq] - lam_ref[:, 0:1] * o_all[:, tq:]
        o = o * lax.rsqrt(jnp.mean(o * o, axis=0, keepdims=True) + SUBLN_EPS)
        o = (o * g_ref[...]) * (1.0 - lam_init)
        rows = pl.ds(pl.multiple_of(qi * tq, tq), tq)
        o_ref[0, 0, rows, :] = (o.T * jax.nn.silu(z_ref[0, 0, rows, :].astype(F32))).astype(BF16)
        return carry

    lax.fori_loop(0, nblk, finish, 0)


def _attn(q, k, v, z2, lam_row, subln_g, lam_init):
    bsz, _, seq, _ = q.shape
    tq = BLK
    nblk = seq // tq
    whole = pl.BlockSpec((1, 1, seq, V_HEAD_DIM), lambda b, h: (b, h, 0, 0))
    return pl.pallas_call(
        functools.partial(_attn_kernel, lam_init=lam_init),
        grid=(bsz, N_HEADS),
        in_specs=[whole, whole, whole, whole, _const_spec((tq, 2 * tq)),
                  _const_spec((1, V_HEAD_DIM)), _const_spec((V_HEAD_DIM, 1))],
        out_specs=whole,
        out_shape=jax.ShapeDtypeStruct(q.shape, BF16),
        scratch_shapes=[
            pltpu.VMEM((nblk, V_HEAD_DIM, 2 * tq), BF16),
            pltpu.VMEM((nblk, V_ROWS, tq), BF16),
            pltpu.VMEM((tq, 2 * tq), F32),
            pltpu.VMEM((tq, 2 * tq), F32),
            pltpu.VMEM((nblk, 8, 2 * tq), F32),
            pltpu.VMEM((nblk, V_ROWS, 2 * tq), F32),
        ],
        compiler_params=_cparams(("parallel", "parallel")),
        name="diff_attn",
    )(q, k, v, z2, _causal_bias(), lam_row, subln_g)


def _final_kernel(o_ref, h_ref, w_ref, g_ref, out_ref):
    y = jnp.concatenate([o_ref[0, hd] for hd in range(N_HEADS)], axis=1)
    h = h_ref[0] + jnp.dot(y, w_ref[...], preferred_element_type=F32)
    out = h * lax.rsqrt(jnp.mean(h * h, axis=-1, keepdims=True) + NORM_EPS) * g_ref[...]
    for j in range(SSM_L):
        out_ref[0, :, j, :] = out[j * BLK_CHUNKS:(j + 1) * BLK_CHUNKS]


def _final(o, h1, out_w, g):
    bsz, seq, _ = h1.shape
    tok = pl.BlockSpec((1, BLK, D_MODEL), lambda b, i: (b, i, 0))
    return pl.pallas_call(
        _final_kernel,
        grid=(bsz, seq // BLK),
        in_specs=[pl.BlockSpec((1, N_HEADS, BLK, V_HEAD_DIM), lambda b, i: (b, 0, i, 0)), tok,
                  _const_spec((D_MODEL, D_MODEL)), _const_spec((1, D_MODEL))],
        out_specs=_NATURAL_TOK,
        out_shape=jax.ShapeDtypeStruct((bsz, seq // SSM_L, SSM_L, D_MODEL), F32),
        compiler_params=_cparams(("parallel", "parallel")),
        name="attn_out_final",
    )(o, h1, out_w, g).reshape(bsz, seq, D_MODEL)


def _rope_tables(seq):
    half = ROT_DIM // 2
    inv = ROPE_THETA ** (-jnp.arange(0, ROT_DIM, 2, dtype=F32) / ROT_DIM)
    row = jnp.arange(seq, dtype=jnp.int32)
    pos = (row // BLK) * BLK + _block_position(row % BLK)
    ang = pos.astype(F32)[:, None] * inv[None, :]
    cos, sin = jnp.cos(ang), jnp.sin(ang)
    ones = jnp.ones((seq, HEAD_DIM - ROT_DIM), F32)
    zeros = jnp.zeros((seq, HEAD_DIM - ROT_DIM), F32)
    zh = jnp.zeros((seq, half), F32)
    cos_h = jnp.concatenate([cos, cos, ones], axis=1)
    lo_h = jnp.concatenate([-sin, zh, zeros], axis=1)
    hi_h = jnp.concatenate([zh, sin, zeros], axis=1)
    rep = LANES // HEAD_DIM
    return jnp.tile(cos_h, (1, rep)), jnp.tile(lo_h, (1, rep)), jnp.tile(hi_h, (1, rep))


def _pick(n, want):
    r = min(n, want)
    assert n % r == 0, (n, want)
    return r


def kernel(x, a_norm_g, a_in_w, a_lambda_re, a_lambda_im, a_log_dt, a_b_re, a_b_im, a_c_re, a_c_im, a_d, a_glu_w, a_glu_b, a_out_w, kv_norm_g, kv_w, b_norm_g, b_in_w, b_lambda_q1, b_lambda_k1, b_lambda_q2, b_lambda_k2, b_subln_g, b_out_w, final_norm_g):
    bsz, seq, d = x.shape
    assert d == D_MODEL and seq % BLK == 0
    nchunk = seq // SSM_L
    x = x.reshape(bsz, nchunk, SSM_L, D_MODEL)

    u, z = _inproj(x, a_norm_g[0][None], a_in_w[0].astype(BF16))
    a_c, c_c, k_c, a_l = _ssm_tables(a_lambda_re[0], a_lambda_im[0], a_log_dt[0], a_b_re[0],
                                     a_b_im[0], a_c_re[0], a_c_im[0], a_d[0])
    y = _ssm(u, a_c, c_c, k_c, a_l, _pick(nchunk, 256))

    cos_t, slo_t, shi_t = _rope_tables(seq)
    h1, q, k, v, z2 = _mid(y, z, x, cos_t, slo_t, shi_t,
                           a_glu_w[0].astype(BF16), a_glu_b[0][None], a_out_w[0].astype(BF16),
                           kv_norm_g[None], kv_w.astype(BF16),
                           b_norm_g[0][None], b_in_w[0].astype(BF16))

    lam_init = 0.8 - 0.6 * math.exp(-0.3 * N_A_LAYERS)
    lam = (jnp.exp(jnp.sum(b_lambda_q1[0] * b_lambda_k1[0]))
           - jnp.exp(jnp.sum(b_lambda_q2[0] * b_lambda_k2[0])) + lam_init)
    lam_row = jnp.full((1, V_HEAD_DIM), lam, F32)
    o = _attn(q, k, v, z2, lam_row, b_subln_g[0][:, None], lam_init)

    return _final(o, h1, b_out_w[0].astype(BF16), final_norm_g[None])
```

```python
import functools
import math

import jax
import jax.numpy as jnp
import numpy as np
from jax import lax
from jax.experimental import pallas as pl
from jax.experimental.pallas import tpu as pltpu

F32 = jnp.float32
BF16 = jnp.bfloat16

D_MODEL = 1024
SSM_GROUPS = 64
SSM_GROUP = 16
SSM_STATE = 64
N_HEADS = 8
HEAD_DIM = 64
V_HEAD_DIM = 128
ROT_DIM = 16
ROPE_THETA = 500000.0
NORM_EPS = 1e-6
SUBLN_EPS = 1e-5
N_A_LAYERS = 1

LANES = 128
SSM_L = 8
GROUPS_PER_TILE = LANES // SSM_GROUP
N_TILES = D_MODEL // LANES
TILE_STATE = GROUPS_PER_TILE * SSM_STATE
BLK = 512
BLK_CHUNKS = BLK // SSM_L
MASK_VALUE = -1e30
Q_SCALE = HEAD_DIM ** -0.5 * math.log2(math.e)
VMEM_LIMIT = 56 * 1024 * 1024


def _cparams(sem):
    return pltpu.CompilerParams(dimension_semantics=sem, vmem_limit_bytes=VMEM_LIMIT)


def _const_spec(shape):
    return pl.BlockSpec(shape, lambda *_: (0,) * len(shape), pipeline_mode=pl.Buffered(1))


def _load_phase_major(x_ref):
    return jnp.concatenate([x_ref[0, :, j, :] for j in range(SSM_L)], axis=0)


_NATURAL_TOK = pl.BlockSpec((1, BLK_CHUNKS, SSM_L, D_MODEL), lambda b, c: (b, c, 0, 0))


def _block_position(row):
    return (row % BLK_CHUNKS) * SSM_L + row // BLK_CHUNKS


def _inproj_kernel(x_ref, g_ref, w_ref, u_ref, z_ref):
    x = _load_phase_major(x_ref)
    xn = x * lax.rsqrt(jnp.mean(x * x, axis=-1, keepdims=True) + NORM_EPS) * g_ref[...]
    uz = jnp.dot(xn.astype(BF16), w_ref[...], preferred_element_type=F32)
    for j in range(SSM_L):
        rows = slice(j * BLK_CHUNKS, (j + 1) * BLK_CHUNKS)
        for kt in range(N_TILES):
            u_ref[0, kt, j] = uz[rows, kt * LANES:(kt + 1) * LANES].astype(BF16)
    z_ref[0] = uz[:, D_MODEL:].astype(BF16)


def _inproj(x, g, w):
    bsz, nchunk, _, _ = x.shape
    seq = nchunk * SSM_L
    tok = pl.BlockSpec((1, BLK, D_MODEL), lambda b, c: (b, c, 0))
    return pl.pallas_call(
        _inproj_kernel,
        grid=(bsz, seq // BLK),
        in_specs=[_NATURAL_TOK, _const_spec((1, D_MODEL)), _const_spec((D_MODEL, 2 * D_MODEL))],
        out_specs=[
            pl.BlockSpec((1, N_TILES, SSM_L, BLK_CHUNKS, LANES), lambda b, c: (b, 0, 0, c, 0)),
            tok,
        ],
        out_shape=[
            jax.ShapeDtypeStruct((bsz, N_TILES, SSM_L, nchunk, LANES), BF16),
            jax.ShapeDtypeStruct((bsz, seq, D_MODEL), BF16),
        ],
        compiler_params=_cparams(("parallel", "parallel")),
        name="s5_inproj",
    )(x, g, w)


def _ssm_tables(lam_re, lam_im, log_dt, b_re, b_im, c_re, c_im, d_skip):
    hi = lax.Precision.HIGHEST
    n_l = SSM_L
    dt = jnp.exp(log_dt)[:, None]
    lr = lam_re * dt
    li = lam_im * dt
    mag = jnp.exp(lr)
    abar_re = mag * jnp.cos(li)
    abar_im = mag * jnp.sin(li)
    den = lam_re * lam_re + lam_im * lam_im
    nr = abar_re - 1.0
    ni = abar_im
    f_re = (nr * lam_re + ni * lam_im) / den
    f_im = (ni * lam_re - nr * lam_im) / den
    bbar_re = f_re[..., None] * b_re - f_im[..., None] * b_im
    bbar_im = f_re[..., None] * b_im + f_im[..., None] * b_re
    n = jnp.arange(n_l + 1, dtype=F32)[:, None, None]
    pmag = jnp.exp(n * lr[None])
    pw_re = pmag * jnp.cos(n * li[None])
    pw_im = pmag * jnp.sin(n * li[None])
    gt, nt = GROUPS_PER_TILE, N_TILES
    bt_re = bbar_re.transpose(0, 2, 1)
    bt_im = bbar_im.transpose(0, 2, 1)
    ab_re = pw_re[:n_l, :, None, :] * bt_re[None] - pw_im[:n_l, :, None, :] * bt_im[None]
    ab_im = pw_re[:n_l, :, None, :] * bt_im[None] + pw_im[:n_l, :, None, :] * bt_re[None]
    kd = (jnp.einsum('gop,tgip->tgio', c_re, ab_re, precision=hi)
          - jnp.einsum('gop,tgip->tgio', c_im, ab_im, precision=hi))
    d_diag = d_skip.reshape(SSM_GROUPS, SSM_GROUP)[:, :, None] * jnp.eye(SSM_GROUP, dtype=F32)
    kd = kd.at[0].add(d_diag)

    def per_tile(a):
        lead, _, r, w = a.shape
        return a.reshape(lead, nt, gt * r, w).transpose(1, 0, 2, 3).reshape(nt, lead * gt * r, w)

    a_c = per_tile(jnp.concatenate([jnp.flip(ab_re, 0), jnp.flip(ab_im, 0)], axis=-1))
    cr = jnp.tile(c_re.transpose(0, 2, 1), (1, 1, n_l))
    ci = jnp.tile(c_im.transpose(0, 2, 1), (1, 1, n_l))
    pr = jnp.repeat(pw_re[1:].transpose(1, 2, 0), SSM_GROUP, axis=-1)
    pi = jnp.repeat(pw_im[1:].transpose(1, 2, 0), SSM_GROUP, axis=-1)
    c_c = per_tile(jnp.stack([cr * pr - ci * pi, -(cr * pi + ci * pr)], axis=0))
    zero = jnp.zeros_like(kd[0])
    k_c = per_tile(jnp.stack(
        [jnp.concatenate([kd[t - j] if j <= t else zero for t in range(n_l)], axis=-1) for j in range(n_l)],
        axis=0))
    a_l = jnp.concatenate([pw_re[n_l].reshape(nt, 1, TILE_STATE),
                           pw_im[n_l].reshape(nt, 1, TILE_STATE)], axis=-1)
    return a_c.astype(BF16), c_c.astype(BF16), k_c.astype(BF16), a_l


def _selection_matrices():
    r = np.arange(2 * SSM_STATE)[:, None]
    c = np.arange(2 * TILE_STATE)[None, :]
    e_in = (r // SSM_STATE == c // TILE_STATE) & (r % SSM_STATE == c % SSM_STATE)
    r = np.arange(SSM_L * SSM_GROUP)[:, None]
    c = np.arange(2 * LANES)[None, :]
    same = ((r // SSM_GROUP) % 2 == c // LANES) & (r % SSM_GROUP == c % SSM_GROUP)
    e_q = np.stack([same & (r // (2 * SSM_GROUP) == q) for q in range(SSM_L // 2)])
    return jnp.asarray(e_in, BF16), jnp.asarray(e_q, BF16)


def _expand_groups(compact, sel, row_group_width, col_group_width):
    full = jnp.dot(compact, sel, preferred_element_type=F32)
    row = lax.broadcasted_iota(jnp.int32, full.shape, 0)
    col = lax.broadcasted_iota(jnp.int32, full.shape, 1)
    gi = (row >> (row_group_width.bit_length() - 1)) & (GROUPS_PER_TILE - 1)
    gj = (col >> (col_group_width.bit_length() - 1)) & (GROUPS_PER_TILE - 1)
    return jnp.where(gi == gj, full, 0.0).astype(BF16)


def _ssm_kernel(u_ref, ac_ref, cc_ref, kc_ref, ein_ref, eq_ref, al_ref, y_ref,
                su_scr, w_scr, carry_scr, win_scr, wst_scr, wu_scr):
    bsz = u_ref.shape[0]
    rows = u_ref.shape[3]
    ts = TILE_STATE

    @pl.when(pl.program_id(1) == 0)
    def _():
        win_scr[...] = _expand_groups(ac_ref[0], ein_ref[...], SSM_GROUP, SSM_STATE)
        for q in range(SSM_L // 2):
            wst_scr[q] = _expand_groups(cc_ref[0], eq_ref[q], SSM_STATE, SSM_GROUP)
            wu_scr[q] = _expand_groups(kc_ref[0], eq_ref[q], SSM_GROUP, SSM_GROUP)
        carry_scr[...] = jnp.zeros_like(carry_scr)

    for b in range(bsz):
        for j in range(SSM_L):
            su_scr[b * rows:(b + 1) * rows, 2 * ts + j * LANES:2 * ts + (j + 1) * LANES] = u_ref[b, 0, j]
    w_scr[...] = jnp.dot(su_scr[:, 2 * ts:], win_scr[...], preferred_element_type=F32)

    a_re = al_ref[0, :, :ts]
    a_im = al_ref[0, :, ts:]

    def body(c, carry):
        out = []
        for b in range(bsz):
            s_re, s_im = carry[2 * b], carry[2 * b + 1]
            row = pl.ds(b * rows + c, 1)
            w_re = w_scr[row, :ts]
            w_im = w_scr[row, ts:]
            w_scr[row, :ts] = s_re
            w_scr[row, ts:] = s_im
            out += [a_re * s_re - a_im * s_im + w_re, a_re * s_im + a_im * s_re + w_im]
        return tuple(out)

    init = []
    for b in range(bsz):
        init += [carry_scr[8 * b:8 * b + 1, :ts], carry_scr[8 * b:8 * b + 1, ts:]]
    final = lax.fori_loop(0, rows, body, tuple(init))
    for b in range(bsz):
        carry_scr[8 * b:8 * b + 1, :ts] = final[2 * b]
        carry_scr[8 * b:8 * b + 1, ts:] = final[2 * b + 1]

    su_scr[:, :2 * ts] = w_scr[...].astype(BF16)
    for q in range(SSM_L // 2):
        k_used = (2 * q + 2) * LANES
        yy = (jnp.dot(su_scr[:, :2 * ts], wst_scr[q], preferred_element_type=F32)
              + jnp.dot(su_scr[:, 2 * ts:2 * ts + k_used], wu_scr[q, :k_used, :],
                        preferred_element_type=F32))
        for b in range(bsz):
            y_ref[b, 0, 2 * q] = yy[b * rows:(b + 1) * rows, :LANES].astype(BF16)
            y_ref[b, 0, 2 * q + 1] = yy[b * rows:(b + 1) * rows, LANES:].astype(BF16)


def _ssm(u, a_c, c_c, k_c, a_l, rows):
    bsz, _, _, nchunk, _ = u.shape
    k_all = 2 * TILE_STATE + SSM_L * LANES
    blk = pl.BlockSpec((bsz, 1, SSM_L, rows, LANES), lambda t, i: (0, t, 0, i, 0))
    return pl.pallas_call(
        _ssm_kernel,
        grid=(N_TILES, nchunk // rows),
        in_specs=[
            blk,
            pl.BlockSpec((1, SSM_L * LANES, 2 * SSM_STATE), lambda t, i: (t, 0, 0)),
            pl.BlockSpec((1, 2 * TILE_STATE, SSM_L * SSM_GROUP), lambda t, i: (t, 0, 0)),
            pl.BlockSpec((1, SSM_L * LANES, SSM_L * SSM_GROUP), lambda t, i: (t, 0, 0)),
            _const_spec((2 * SSM_STATE, 2 * TILE_STATE)),
            _const_spec((SSM_L // 2, SSM_L * SSM_GROUP, 2 * LANES)),
            pl.BlockSpec((1, 1, 2 * TILE_STATE), lambda t, i: (t, 0, 0)),
        ],
        out_specs=blk,
        out_shape=jax.ShapeDtypeStruct(u.shape, BF16),
        scratch_shapes=[
            pltpu.VMEM((bsz * rows, k_all), BF16),
            pltpu.VMEM((bsz * rows, 2 * TILE_STATE), F32),
            pltpu.VMEM((8 * bsz, 2 * TILE_STATE), F32),
            pltpu.VMEM((SSM_L * LANES, 2 * TILE_STATE), BF16),
            pltpu.VMEM((SSM_L // 2, 2 * TILE_STATE, 2 * LANES), BF16),
            pltpu.VMEM((SSM_L // 2, SSM_L * LANES, 2 * LANES), BF16),
        ],
        compiler_params=_cparams(("parallel", "arbitrary")),
        name="s5_scan",
    )(u, a_c, c_c, k_c, *_selection_matrices(), a_l)


def _rope(t, cos_f, sin_lo, sin_hi):
    half = ROT_DIM // 2
    width = t.shape[1]
    return (t * cos_f + pltpu.roll(t, half, 1) * sin_hi
            + pltpu.roll(t, width - half, 1) * sin_lo)


def _mid_kernel(y_ref, z_ref, x_ref, cos_ref, slo_ref, shi_ref, gluw_ref, glub_ref, outw_ref,
                kvg_ref, kvw_ref, bg_ref, binw_ref,
                h_ref, q_ref, k_ref, v_ref, z2_ref):
    y = jnp.concatenate(
        [jnp.concatenate([y_ref[0, kt, j] for kt in range(N_TILES)], axis=1) for j in range(SSM_L)],
        axis=0).astype(F32)
    y = 0.5 * y * (1.0 + lax.erf(y * (2.0 ** -0.5)))
    gate = jnp.dot(y.astype(BF16), gluw_ref[...], preferred_element_type=F32) + glub_ref[...]
    y = y * jax.nn.sigmoid(gate)
    y = y * jax.nn.silu(z_ref[0].astype(F32))
    h = _load_phase_major(x_ref) + jnp.dot(y.astype(BF16), outw_ref[...], preferred_element_type=F32)
    h_ref[0] = h

    hn = h * lax.rsqrt(jnp.mean(h * h, axis=-1, keepdims=True) + NORM_EPS)
    reps = D_MODEL // LANES
    cos_f = jnp.concatenate([cos_ref[...]] * reps, axis=1)
    sin_lo = jnp.concatenate([slo_ref[...]] * reps, axis=1)
    sin_hi = jnp.concatenate([shi_ref[...]] * reps, axis=1)

    kv = jnp.dot((hn * kvg_ref[...]).astype(BF16), kvw_ref[...], preferred_element_type=F32)
    k = _rope(kv[:, :D_MODEL], cos_f, sin_lo, sin_hi).astype(BF16)
    v = kv[:, D_MODEL:].astype(BF16)
    qz = jnp.dot((hn * bg_ref[...]).astype(BF16), binw_ref[...], preferred_element_type=F32)
    q = (_rope(qz[:, :D_MODEL], cos_f, sin_lo, sin_hi) * Q_SCALE).astype(BF16)
    z2 = qz[:, D_MODEL:].astype(BF16)
    for hd in range(N_HEADS):
        cols = slice(hd * V_HEAD_DIM, (hd + 1) * V_HEAD_DIM)
        q_ref[0, hd] = q[:, cols]
        k_ref[0, hd] = k[:, cols]
        v_ref[0, hd] = v[:, cols]
        z2_ref[0, hd] = z2[:, cols]


def _mid(y, z, x, cos_t, slo_t, shi_t, glu_w, glu_b, out_w, kv_g, kv_w, b_g, bin_w):
    bsz, nchunk, _, _ = x.shape
    seq = nchunk * SSM_L
    tok = pl.BlockSpec((1, BLK, D_MODEL), lambda b, c: (b, c, 0))
    tab = pl.BlockSpec((BLK, LANES), lambda b, c: (c, 0))
    heads = pl.BlockSpec((1, N_HEADS, BLK, V_HEAD_DIM), lambda b, c: (b, 0, c, 0))
    per_head = jax.ShapeDtypeStruct((bsz, N_HEADS, seq, V_HEAD_DIM), BF16)
    return pl.pallas_call(
        _mid_kernel,
        grid=(bsz, seq // BLK),
        in_specs=[
            pl.BlockSpec((1, N_TILES, SSM_L, BLK_CHUNKS, LANES), lambda b, c: (b, 0, 0, c, 0)),
            tok, _NATURAL_TOK, tab, tab, tab,
            _const_spec((D_MODEL, D_MODEL)), _const_spec((1, D_MODEL)), _const_spec((D_MODEL, D_MODEL)),
            _const_spec((1, D_MODEL)), _const_spec((D_MODEL, 2 * D_MODEL)),
            _const_spec((1, D_MODEL)), _const_spec((D_MODEL, 2 * D_MODEL)),
        ],
        out_specs=[tok, heads, heads, heads, heads],
        out_shape=[jax.ShapeDtypeStruct((bsz, seq, D_MODEL), F32), per_head, per_head, per_head, per_head],
        compiler_params=_cparams(("parallel", "parallel")),
        name="s5_out_qkv",
    )(y, z, x, cos_t, slo_t, shi_t, glu_w, glu_b, out_w, kv_g, kv_w, b_g, bin_w)


V_ROWS = V_HEAD_DIM + 16
WALK_UNROLL_BELOW = 24
WALK_UNROLL_DIAG = 16


def _causal_bias():
    pos = _block_position(np.arange(BLK))
    visible = pos[:, None] <= np.concatenate([pos, pos])[None, :]
    return jnp.asarray(np.where(visible, 0.0, MASK_VALUE), F32)


def _attn_kernel(q_ref, k_ref, v_ref, z_ref, bias_ref, lam_ref, g_ref, o_ref,
                 qs_scr, vt_scr, sa_scr, sb_scr, m_scr, acc_scr, *, lam_init):
    tq = BLK
    nblk = k_ref.shape[2] // tq

    for c in range(nblk):
        rows = slice(c * tq, (c + 1) * tq)
        vt_scr[c, :V_HEAD_DIM, :] = v_ref[0, 0, rows, :].astype(F32).T.astype(BF16)
        one_row = lax.broadcasted_iota(jnp.int32, (V_ROWS - V_HEAD_DIM, tq), 0) == 0
        vt_scr[c, V_HEAD_DIM:, :] = jnp.where(one_row, 1.0, 0.0).astype(BF16)
        qt = q_ref[0, 0, rows, :].astype(F32).T
        dim = lax.broadcasted_iota(jnp.int32, qt.shape, 0)
        qs_scr[c, :, :tq] = jnp.where(dim < HEAD_DIM, qt, 0.0).astype(BF16)
        qs_scr[c, :, tq:] = jnp.where(dim >= HEAD_DIM, qt, 0.0).astype(BF16)
    m_scr[...] = jnp.full_like(m_scr, MASK_VALUE)
    acc_scr[...] = jnp.zeros_like(acc_scr)

    def scores(qi, kb, s_scr, diagonal):
        start = pl.multiple_of(kb * tq, tq)
        k = k_ref[0, 0, pl.ds(start, tq), :]
        s = jnp.dot(k, qs_scr[qi], preferred_element_type=F32)
        s_scr[...] = s + bias_ref[...] if diagonal else s

    def update(qi, kb, s_scr):
        s = s_scr[...]
        m_prev = m_scr[qi, 0:1, :]
        m_new = jnp.maximum(m_prev, jnp.max(s, axis=0, keepdims=True))
        alpha = jnp.exp2(m_prev - m_new)
        p = jnp.exp2(s - m_new).astype(BF16)
        pv = jnp.dot(vt_scr[kb], p, preferred_element_type=F32)
        acc_scr[qi] = alpha * acc_scr[qi] + pv
        m_scr[qi, 0:1, :] = m_new

    def walk(count, first, following, diagonal, unroll):
        if count == 0:
            return
        last_blk = nblk - 1
        scores(first[0], first[1], sa_scr, diagonal)

        def steps(n):
            def body(_, cur):
                for step in range(n):
                    here, there = (sa_scr, sb_scr) if step % 2 == 0 else (sb_scr, sa_scr)
                    nxt = following(*cur)
                    scores(jnp.minimum(nxt[0], last_blk), jnp.minimum(nxt[1], last_blk), there, diagonal)
                    update(cur[0], cur[1], here)
                    cur = nxt
                return cur
            return body

        cur = (jnp.int32(first[0]), jnp.int32(first[1]))
        cur = lax.fori_loop(0, count // unroll, steps(unroll), cur)
        for _ in range(count % unroll // 2):
            cur = steps(2)(0, cur)
        if count % 2:
            update(cur[0], cur[1], sa_scr)

    def below_next(qi, kb):
        last = kb == qi - 1
        return jnp.where(last, qi + 1, qi), jnp.where(last, 0, kb + 1)

    walk(nblk * (nblk - 1) // 2, (1, 0), below_next, False, WALK_UNROLL_BELOW)
    walk(nblk, (0, 0), lambda qi, kb: (qi + 1, kb + 1), True, WALK_UNROLL_DIAG)

    def finish(qi, carry):
        acc = acc_scr[qi]
        o_all = acc[:V_HEAD_DIM] * (1.0 / acc[V_HEAD_DIM:V_HEAD_DIM + 1])
        o = o_all[:, :tq] - lam_ref[:, 0:1] * o_all[:, tq:]
        o = o * lax.rsqrt(jnp.mean(o * o, axis=0, keepdims=True) + SUBLN_EPS)
        o = (o * g_ref[...]) * (1.0 - lam_init)
        rows = pl.ds(pl.multiple_of(qi * tq, tq), tq)
        o_ref[0, 0, rows, :] = (o.T * jax.nn.silu(z_ref[0, 0, rows, :].astype(F32))).astype(BF16)
        return carry

    lax.fori_loop(0, nblk, finish, 0)


def _attn(q, k, v, z2, lam_row, subln_g, lam_init):
    bsz, _, seq, _ = q.shape
    tq = BLK
    nblk = seq // tq
    whole = pl.BlockSpec((1, 1, seq, V_HEAD_DIM), lambda b, h: (b, h, 0, 0))
    return pl.pallas_call(
        functools.partial(_attn_kernel, lam_init=lam_init),
        grid=(bsz, N_HEADS),
        in_specs=[whole, whole, whole, whole, _const_spec((tq, 2 * tq)),
                  _const_spec((1, V_HEAD_DIM)), _const_spec((V_HEAD_DIM, 1))],
        out_specs=whole,
        out_shape=jax.ShapeDtypeStruct(q.shape, BF16),
        scratch_shapes=[
            pltpu.VMEM((nblk, V_HEAD_DIM, 2 * tq), BF16),
            pltpu.VMEM((nblk, V_ROWS, tq), BF16),
            pltpu.VMEM((tq, 2 * tq), F32),
            pltpu.VMEM((tq, 2 * tq), F32),
            pltpu.VMEM((nblk, 8, 2 * tq), F32),
            pltpu.VMEM((nblk, V_ROWS, 2 * tq), F32),
        ],
        compiler_params=_cparams(("parallel", "parallel")),
        name="diff_attn",
    )(q, k, v, z2, _causal_bias(), lam_row, subln_g)


def _final_kernel(o_ref, h_ref, w_ref, g_ref, out_ref):
    y = jnp.concatenate([o_ref[0, hd] for hd in range(N_HEADS)], axis=1)
    h = h_ref[0] + jnp.dot(y, w_ref[...], preferred_element_type=F32)
    out = h * lax.rsqrt(jnp.mean(h * h, axis=-1, keepdims=True) + NORM_EPS) * g_ref[...]
    for j in range(SSM_L):
        out_ref[0, :, j, :] = out[j * BLK_CHUNKS:(j + 1) * BLK_CHUNKS]


def _final(o, h1, out_w, g):
    bsz, seq, _ = h1.shape
    tok = pl.BlockSpec((1, BLK, D_MODEL), lambda b, i: (b, i, 0))
    return pl.pallas_call(
        _final_kernel,
        grid=(bsz, seq // BLK),
        in_specs=[pl.BlockSpec((1, N_HEADS, BLK, V_HEAD_DIM), lambda b, i: (b, 0, i, 0)), tok,
                  _const_spec((D_MODEL, D_MODEL)), _const_spec((1, D_MODEL))],
        out_specs=_NATURAL_TOK,
        out_shape=jax.ShapeDtypeStruct((bsz, seq // SSM_L, SSM_L, D_MODEL), F32),
        compiler_params=_cparams(("parallel", "parallel")),
        name="attn_out_final",
    )(o, h1, out_w, g).reshape(bsz, seq, D_MODEL)


def _rope_tables(seq):
    half = ROT_DIM // 2
    inv = ROPE_THETA ** (-jnp.arange(0, ROT_DIM, 2, dtype=F32) / ROT_DIM)
    row = jnp.arange(seq, dtype=jnp.int32)
    pos = (row // BLK) * BLK + _block_position(row % BLK)
    ang = pos.astype(F32)[:, None] * inv[None, :]
    cos, sin = jnp.cos(ang), jnp.sin(ang)
    ones = jnp.ones((seq, HEAD_DIM - ROT_DIM), F32)
    zeros = jnp.zeros((seq, HEAD_DIM - ROT_DIM), F32)
    zh = jnp.zeros((seq, half), F32)
    cos_h = jnp.concatenate([cos, cos, ones], axis=1)
    lo_h = jnp.concatenate([-sin, zh, zeros], axis=1)
    hi_h = jnp.concatenate([zh, sin, zeros], axis=1)
    rep = LANES // HEAD_DIM
    return jnp.tile(cos_h, (1, rep)), jnp.tile(lo_h, (1, rep)), jnp.tile(hi_h, (1, rep))


def _pick(n, want):
    r = min(n, want)
    assert n % r == 0, (n, want)
    return r


def kernel(x, a_norm_g, a_in_w, a_lambda_re, a_lambda_im, a_log_dt, a_b_re, a_b_im, a_c_re, a_c_im, a_d, a_glu_w, a_glu_b, a_out_w, kv_norm_g, kv_w, b_norm_g, b_in_w, b_lambda_q1, b_lambda_k1, b_lambda_q2, b_lambda_k2, b_subln_g, b_out_w, final_norm_g):
    bsz, seq, d = x.shape
    assert d == D_MODEL and seq % BLK == 0
    nchunk = seq // SSM_L
    x = x.reshape(bsz, nchunk, SSM_L, D_MODEL)

    u, z = _inproj(x, a_norm_g[0][None], a_in_w[0].astype(BF16))
    a_c, c_c, k_c, a_l = _ssm_tables(a_lambda_re[0], a_lambda_im[0], a_log_dt[0], a_b_re[0],
                                     a_b_im[0], a_c_re[0], a_c_im[0], a_d[0])
    y = _ssm(u, a_c, c_c, k_c, a_l, _pick(nchunk, 256))

    cos_t, slo_t, shi_t = _rope_tables(seq)
    h1, q, k, v, z2 = _mid(y, z, x, cos_t, slo_t, shi_t,
                           a_glu_w[0].astype(BF16), a_glu_b[0][None], a_out_w[0].astype(BF16),
                           kv_norm_g[None], kv_w.astype(BF16),
                           b_norm_g[0][None], b_in_w[0].astype(BF16))

    lam_init = 0.8 - 0.6 * math.exp(-0.3 * N_A_LAYERS)
    lam = (jnp.exp(jnp.sum(b_lambda_q1[0] * b_lambda_k1[0]))
           - jnp.exp(jnp.sum(b_lambda_q2[0] * b_lambda_k2[0])) + lam_init)
    lam_row = jnp.full((1, V_HEAD_DIM), lam, F32)
    o = _attn(q, k, v, z2, lam_row, b_subln_g[0][:, None], lam_init)

    return _final(o, h1, b_out_w[0].astype(BF16), final_norm_g[None])
```

```python
import functools
import math

import jax
import jax.numpy as jnp
import numpy as np
from jax import lax
from jax.experimental import pallas as pl
from jax.experimental.pallas import tpu as pltpu

F32 = jnp.float32
BF16 = jnp.bfloat16

D_MODEL = 1024
SSM_GROUPS = 64
SSM_GROUP = 16
SSM_STATE = 64
N_HEADS = 8
HEAD_DIM = 64
V_HEAD_DIM = 128
ROT_DIM = 16
ROPE_THETA = 500000.0
NORM_EPS = 1e-6
SUBLN_EPS = 1e-5
N_A_LAYERS = 1

LANES = 128
SSM_L = 8
GROUPS_PER_TILE = LANES // SSM_GROUP
N_TILES = D_MODEL // LANES
TILE_STATE = GROUPS_PER_TILE * SSM_STATE
BLK = 512
BLK_CHUNKS = BLK // SSM_L
MASK_VALUE = -1e30
Q_SCALE = HEAD_DIM ** -0.5 * math.log2(math.e)
VMEM_LIMIT = 56 * 1024 * 1024


def _cparams(sem):
    return pltpu.CompilerParams(dimension_semantics=sem, vmem_limit_bytes=VMEM_LIMIT)


def _const_spec(shape):
    return pl.BlockSpec(shape, lambda *_: (0,) * len(shape), pipeline_mode=pl.Buffered(1))


def _load_phase_major(x_ref):
    return jnp.concatenate([x_ref[0, :, j, :] for j in range(SSM_L)], axis=0)


_NATURAL_TOK = pl.BlockSpec((1, BLK_CHUNKS, SSM_L, D_MODEL), lambda b, c: (b, c, 0, 0))


def _block_position(row):
    return (row % BLK_CHUNKS) * SSM_L + row // BLK_CHUNKS


def _inproj_kernel(x_ref, g_ref, w_ref, u_ref, z_ref):
    x = _load_phase_major(x_ref)
    xn = x * lax.rsqrt(jnp.mean(x * x, axis=-1, keepdims=True) + NORM_EPS) * g_ref[...]
    uz = jnp.dot(xn.astype(BF16), w_ref[...], preferred_element_type=F32)
    for j in range(SSM_L):
        rows = slice(j * BLK_CHUNKS, (j + 1) * BLK_CHUNKS)
        for kt in range(N_TILES):
            u_ref[0, kt, j] = uz[rows, kt * LANES:(kt + 1) * LANES].astype(BF16)
    z_ref[0] = uz[:, D_MODEL:].astype(BF16)


def _inproj(x, g, w):
    bsz, nchunk, _, _ = x.shape
    seq = nchunk * SSM_L
    tok = pl.BlockSpec((1, BLK, D_MODEL), lambda b, c: (b, c, 0))
    return pl.pallas_call(
        _inproj_kernel,
        grid=(bsz, seq // BLK),
        in_specs=[_NATURAL_TOK, _const_spec((1, D_MODEL)), _const_spec((D_MODEL, 2 * D_MODEL))],
        out_specs=[
            pl.BlockSpec((1, N_TILES, SSM_L, BLK_CHUNKS, LANES), lambda b, c: (b, 0, 0, c, 0)),
            tok,
        ],
        out_shape=[
            jax.ShapeDtypeStruct((bsz, N_TILES, SSM_L, nchunk, LANES), BF16),
            jax.ShapeDtypeStruct((bsz, seq, D_MODEL), BF16),
        ],
        compiler_params=_cparams(("parallel", "parallel")),
        name="s5_inproj",
    )(x, g, w)


def _ssm_tables(lam_re, lam_im, log_dt, b_re, b_im, c_re, c_im, d_skip):
    hi = lax.Precision.HIGHEST
    n_l = SSM_L
    dt = jnp.exp(log_dt)[:, None]
    lr = lam_re * dt
    li = lam_im * dt
    mag = jnp.exp(lr)
    abar_re = mag * jnp.cos(li)
    abar_im = mag * jnp.sin(li)
    den = lam_re * lam_re + lam_im * lam_im
    nr = abar_re - 1.0
    ni = abar_im
    f_re = (nr * lam_re + ni * lam_im) / den
    f_im = (ni * lam_re - nr * lam_im) / den
    bbar_re = f_re[..., None] * b_re - f_im[..., None] * b_im
    bbar_im = f_re[..., None] * b_im + f_im[..., None] * b_re
    n = jnp.arange(n_l + 1, dtype=F32)[:, None, None]
    pmag = jnp.exp(n * lr[None])
    pw_re = pmag * jnp.cos(n * li[None])
    pw_im = pmag * jnp.sin(n * li[None])
    gt, nt = GROUPS_PER_TILE, N_TILES
    bt_re = bbar_re.transpose(0, 2, 1)
    bt_im = bbar_im.transpose(0, 2, 1)
    ab_re = pw_re[:n_l, :, None, :] * bt_re[None] - pw_im[:n_l, :, None, :] * bt_im[None]
    ab_im = pw_re[:n_l, :, None, :] * bt_im[None] + pw_im[:n_l, :, None, :] * bt_re[None]
    kd = (jnp.einsum('gop,tgip->tgio', c_re, ab_re, precision=hi)
          - jnp.einsum('gop,tgip->tgio', c_im, ab_im, precision=hi))
    d_diag = d_skip.reshape(SSM_GROUPS, SSM_GROUP)[:, :, None] * jnp.eye(SSM_GROUP, dtype=F32)
    kd = kd.at[0].add(d_diag)

    def per_tile(a):
        lead, _, r, w = a.shape
        return a.reshape(lead, nt, gt * r, w).transpose(1, 0, 2, 3).reshape(nt, lead * gt * r, w)

    a_c = per_tile(jnp.concatenate([jnp.flip(ab_re, 0), jnp.flip(ab_im, 0)], axis=-1))
    cr = jnp.tile(c_re.transpose(0, 2, 1), (1, 1, n_l))
    ci = jnp.tile(c_im.transpose(0, 2, 1), (1, 1, n_l))
    pr = jnp.repeat(pw_re[1:].transpose(1, 2, 0), SSM_GROUP, axis=-1)
    pi = jnp.repeat(pw_im[1:].transpose(1, 2, 0), SSM_GROUP, axis=-1)
    c_c = per_tile(jnp.stack([cr * pr - ci * pi, -(cr * pi + ci * pr)], axis=0))
    zero = jnp.zeros_like(kd[0])
    k_c = per_tile(jnp.stack(
        [jnp.concatenate([kd[t - j] if j <= t else zero for t in range(n_l)], axis=-1) for j in range(n_l)],
        axis=0))
    a_l = jnp.concatenate([pw_re[n_l].reshape(nt, 1, TILE_STATE),
                           pw_im[n_l].reshape(nt, 1, TILE_STATE)], axis=-1)
    return a_c.astype(BF16), c_c.astype(BF16), k_c.astype(BF16), a_l


def _selection_matrices():
    r = np.arange(2 * SSM_STATE)[:, None]
    c = np.arange(2 * TILE_STATE)[None, :]
    e_in = (r // SSM_STATE == c // TILE_STATE) & (r % SSM_STATE == c % SSM_STATE)
    r = np.arange(SSM_L * SSM_GROUP)[:, None]
    c = np.arange(2 * LANES)[None, :]
    same = ((r // SSM_GROUP) % 2 == c // LANES) & (r % SSM_GROUP == c % SSM_GROUP)
    e_q = np.stack([same & (r // (2 * SSM_GROUP) == q) for q in range(SSM_L // 2)])
    return jnp.asarray(e_in, BF16), jnp.asarray(e_q, BF16)


def _expand_groups(compact, sel, row_group_width, col_group_width):
    full = jnp.dot(compact, sel, preferred_element_type=F32)
    row = lax.broadcasted_iota(jnp.int32, full.shape, 0)
    col = lax.broadcasted_iota(jnp.int32, full.shape, 1)
    gi = (row >> (row_group_width.bit_length() - 1)) & (GROUPS_PER_TILE - 1)
    gj = (col >> (col_group_width.bit_length() - 1)) & (GROUPS_PER_TILE - 1)
    return jnp.where(gi == gj, full, 0.0).astype(BF16)


def _ssm_kernel(u_ref, ac_ref, cc_ref, kc_ref, ein_ref, eq_ref, al_ref, y_ref,
                su_scr, w_scr, carry_scr, win_scr, wst_scr, wu_scr):
    bsz = u_ref.shape[0]
    rows = u_ref.shape[3]
    ts = TILE_STATE

    @pl.when(pl.program_id(1) == 0)
    def _():
        win_scr[...] = _expand_groups(ac_ref[0], ein_ref[...], SSM_GROUP, SSM_STATE)
        for q in range(SSM_L // 2):
            wst_scr[q] = _expand_groups(cc_ref[0], eq_ref[q], SSM_STATE, SSM_GROUP)
            wu_scr[q] = _expand_groups(kc_ref[0], eq_ref[q], SSM_GROUP, SSM_GROUP)
        carry_scr[...] = jnp.zeros_like(carry_scr)

    for b in range(bsz):
        for j in range(SSM_L):
            su_scr[b * rows:(b + 1) * rows, 2 * ts + j * LANES:2 * ts + (j + 1) * LANES] = u_ref[b, 0, j]
    w_scr[...] = jnp.dot(su_scr[:, 2 * ts:], win_scr[...], preferred_element_type=F32)

    a_re = al_ref[0, :, :ts]
    a_im = al_ref[0, :, ts:]

    def body(c, carry):
        out = []
        for b in range(bsz):
            s_re, s_im = carry[2 * b], carry[2 * b + 1]
            row = pl.ds(b * rows + c, 1)
            w_re = w_scr[row, :ts]
            w_im = w_scr[row, ts:]
            w_scr[row, :ts] = s_re
            w_scr[row, ts:] = s_im
            out += [a_re * s_re - a_im * s_im + w_re, a_re * s_im + a_im * s_re + w_im]
        return tuple(out)

    init = []
    for b in range(bsz):
        init += [carry_scr[8 * b:8 * b + 1, :ts], carry_scr[8 * b:8 * b + 1, ts:]]
    final = lax.fori_loop(0, rows, body, tuple(init))
    for b in range(bsz):
        carry_scr[8 * b:8 * b + 1, :ts] = final[2 * b]
        carry_scr[8 * b:8 * b + 1, ts:] = final[2 * b + 1]

    su_scr[:, :2 * ts] = w_scr[...].astype(BF16)
    for q in range(SSM_L // 2):
        k_used = (2 * q + 2) * LANES
        yy = (jnp.dot(su_scr[:, :2 * ts], wst_scr[q], preferred_element_type=F32)
              + jnp.dot(su_scr[:, 2 * ts:2 * ts + k_used], wu_scr[q, :k_used, :],
                        preferred_element_type=F32))
        for b in range(bsz):
            y_ref[b, 0, 2 * q] = yy[b * rows:(b + 1) * rows, :LANES].astype(BF16)
            y_ref[b, 0, 2 * q + 1] = yy[b * rows:(b + 1) * rows, LANES:].astype(BF16)


def _ssm(u, a_c, c_c, k_c, a_l, rows):
    bsz, _, _, nchunk, _ = u.shape
    k_all = 2 * TILE_STATE + SSM_L * LANES
    blk = pl.BlockSpec((bsz, 1, SSM_L, rows, LANES), lambda t, i: (0, t, 0, i, 0))
    return pl.pallas_call(
        _ssm_kernel,
        grid=(N_TILES, nchunk // rows),
        in_specs=[
            blk,
            pl.BlockSpec((1, SSM_L * LANES, 2 * SSM_STATE), lambda t, i: (t, 0, 0)),
            pl.BlockSpec((1, 2 * TILE_STATE, SSM_L * SSM_GROUP), lambda t, i: (t, 0, 0)),
            pl.BlockSpec((1, SSM_L * LANES, SSM_L * SSM_GROUP), lambda t, i: (t, 0, 0)),
            _const_spec((2 * SSM_STATE, 2 * TILE_STATE)),
            _const_spec((SSM_L // 2, SSM_L * SSM_GROUP, 2 * LANES)),
            pl.BlockSpec((1, 1, 2 * TILE_STATE), lambda t, i: (t, 0, 0)),
        ],
        out_specs=blk,
        out_shape=jax.ShapeDtypeStruct(u.shape, BF16),
        scratch_shapes=[
            pltpu.VMEM((bsz * rows, k_all), BF16),
            pltpu.VMEM((bsz * rows, 2 * TILE_STATE), F32),
            pltpu.VMEM((8 * bsz, 2 * TILE_STATE), F32),
            pltpu.VMEM((SSM_L * LANES, 2 * TILE_STATE), BF16),
            pltpu.VMEM((SSM_L // 2, 2 * TILE_STATE, 2 * LANES), BF16),
            pltpu.VMEM((SSM_L // 2, SSM_L * LANES, 2 * LANES), BF16),
        ],
        compiler_params=_cparams(("parallel", "arbitrary")),
        name="s5_scan",
    )(u, a_c, c_c, k_c, *_selection_matrices(), a_l)


def _rope(t, cos_f, sin_lo, sin_hi):
    half = ROT_DIM // 2
    width = t.shape[1]
    return (t * cos_f + pltpu.roll(t, half, 1) * sin_hi
            + pltpu.roll(t, width - half, 1) * sin_lo)


def _mid_kernel(y_ref, z_ref, x_ref, cos_ref, slo_ref, shi_ref, gluw_ref, glub_ref, outw_ref,
                kvg_ref, kvw_ref, bg_ref, binw_ref,
                h_ref, q_ref, k_ref, v_ref, z2_ref):
    y = jnp.concatenate(
        [jnp.concatenate([y_ref[0, kt, j] for kt in range(N_TILES)], axis=1) for j in range(SSM_L)],
        axis=0).astype(F32)
    y = 0.5 * y * (1.0 + lax.erf(y * (2.0 ** -0.5)))
    gate = jnp.dot(y.astype(BF16), gluw_ref[...], preferred_element_type=F32) + glub_ref[...]
    y = y * jax.nn.sigmoid(gate)
    y = y * jax.nn.silu(z_ref[0].astype(F32))
    h = _load_phase_major(x_ref) + jnp.dot(y.astype(BF16), outw_ref[...], preferred_element_type=F32)
    h_ref[0] = h

    hn = h * lax.rsqrt(jnp.mean(h * h, axis=-1, keepdims=True) + NORM_EPS)
    reps = D_MODEL // LANES
    cos_f = jnp.concatenate([cos_ref[...]] * reps, axis=1)
    sin_lo = jnp.concatenate([slo_ref[...]] * reps, axis=1)
    sin_hi = jnp.concatenate([shi_ref[...]] * reps, axis=1)

    kv = jnp.dot((hn * kvg_ref[...]).astype(BF16), kvw_ref[...], preferred_element_type=F32)
    k = _rope(kv[:, :D_MODEL], cos_f, sin_lo, sin_hi).astype(BF16)
    v = kv[:, D_MODEL:].astype(BF16)
    qz = jnp.dot((hn * bg_ref[...]).astype(BF16), binw_ref[...], preferred_element_type=F32)
    q = (_rope(qz[:, :D_MODEL], cos_f, sin_lo, sin_hi) * Q_SCALE).astype(BF16)
    z2 = qz[:, D_MODEL:].astype(BF16)
    for hd in range(N_HEADS):
        cols = slice(hd * V_HEAD_DIM, (hd + 1) * V_HEAD_DIM)
        q_ref[0, hd] = q[:, cols]
        k_ref[0, hd] = k[:, cols]
        v_ref[0, hd] = v[:, cols]
        z2_ref[0, hd] = z2[:, cols]


def _mid(y, z, x, cos_t, slo_t, shi_t, glu_w, glu_b, out_w, kv_g, kv_w, b_g, bin_w):
    bsz, nchunk, _, _ = x.shape
    seq = nchunk * SSM_L
    tok = pl.BlockSpec((1, BLK, D_MODEL), lambda b, c: (b, c, 0))
    tab = pl.BlockSpec((BLK, LANES), lambda b, c: (c, 0))
    heads = pl.BlockSpec((1, N_HEADS, BLK, V_HEAD_DIM), lambda b, c: (b, 0, c, 0))
    per_head = jax.ShapeDtypeStruct((bsz, N_HEADS, seq, V_HEAD_DIM), BF16)
    return pl.pallas_call(
        _mid_kernel,
        grid=(bsz, seq // BLK),
        in_specs=[
            pl.BlockSpec((1, N_TILES, SSM_L, BLK_CHUNKS, LANES), lambda b, c: (b, 0, 0, c, 0)),
            tok, _NATURAL_TOK, tab, tab, tab,
            _const_spec((D_MODEL, D_MODEL)), _const_spec((1, D_MODEL)), _const_spec((D_MODEL, D_MODEL)),
            _const_spec((1, D_MODEL)), _const_spec((D_MODEL, 2 * D_MODEL)),
            _const_spec((1, D_MODEL)), _const_spec((D_MODEL, 2 * D_MODEL)),
        ],
        out_specs=[tok, heads, heads, heads, heads],
        out_shape=[jax.ShapeDtypeStruct((bsz, seq, D_MODEL), F32), per_head, per_head, per_head, per_head],
        compiler_params=_cparams(("parallel", "parallel")),
        name="s5_out_qkv",
    )(y, z, x, cos_t, slo_t, shi_t, glu_w, glu_b, out_w, kv_g, kv_w, b_g, bin_w)


V_ROWS = V_HEAD_DIM + 16
WALK_UNROLL_BELOW = 24
WALK_UNROLL_DIAG = 16


def _causal_bias():
    pos = _block_position(np.arange(BLK))
    visible = pos[:, None] <= np.concatenate([pos, pos])[None, :]
    return jnp.asarray(np.where(visible, 0.0, MASK_VALUE), F32)


def _attn_kernel(q_ref, k_ref, v_ref, z_ref, bias_ref, lam_ref, g_ref, o_ref,
                 qs_scr, vt_scr, sa_scr, sb_scr, bm_scr, m_scr, acc_scr, *, lam_init):
    tq = BLK
    nblk = k_ref.shape[2] // tq

    for c in range(nblk):
        rows = slice(c * tq, (c + 1) * tq)
        vt_scr[c, :V_HEAD_DIM, :] = v_ref[0, 0, rows, :].astype(F32).T.astype(BF16)
        one_row = lax.broadcasted_iota(jnp.int32, (V_ROWS - V_HEAD_DIM, tq), 0) == 0
        vt_scr[c, V_HEAD_DIM:, :] = jnp.where(one_row, 1.0, 0.0).astype(BF16)
        qt = q_ref[0, 0, rows, :].astype(F32).T
        dim = lax.broadcasted_iota(jnp.int32, qt.shape, 0)
        qs_scr[c, :, :tq] = jnp.where(dim < HEAD_DIM, qt, 0.0).astype(BF16)
        qs_scr[c, :, tq:] = jnp.where(dim >= HEAD_DIM, qt, 0.0).astype(BF16)
    m_scr[...] = jnp.full_like(m_scr, MASK_VALUE)
    acc_scr[...] = jnp.zeros_like(acc_scr)

    def scores(qi, kb, s_scr, diagonal):
        start = pl.multiple_of(kb * tq, tq)
        k = k_ref[0, 0, pl.ds(start, tq), :]
        s = jnp.dot(k, qs_scr[qi], preferred_element_type=F32)
        s = s + bias_ref[...] if diagonal else s
        s_scr[...] = s
        bm_scr[0 if s_scr is sa_scr else 1, 0:1, :] = jnp.max(s, axis=0, keepdims=True)

    def update(qi, kb, s_scr):
        s = s_scr[...]
        m_prev = m_scr[qi, 0:1, :]
        m_new = jnp.maximum(m_prev, bm_scr[0 if s_scr is sa_scr else 1, 0:1, :])
        alpha = jnp.exp2(m_prev - m_new)
        p = jnp.exp2(s - m_new).astype(BF16)
        pv = jnp.dot(vt_scr[kb], p, preferred_element_type=F32)
        acc_scr[qi] = alpha * acc_scr[qi] + pv
        m_scr[qi, 0:1, :] = m_new

    def walk(count, first, following, diagonal, unroll):
        if count == 0:
            return
        last_blk = nblk - 1
        scores(first[0], first[1], sa_scr, diagonal)

        def steps(n):
            def body(_, cur):
                for step in range(n):
                    here, there = (sa_scr, sb_scr) if step % 2 == 0 else (sb_scr, sa_scr)
                    nxt = following(*cur)
                    scores(jnp.minimum(nxt[0], last_blk), jnp.minimum(nxt[1], last_blk), there, diagonal)
                    update(cur[0], cur[1], here)
                    cur = nxt
                return cur
            return body

        cur = (jnp.int32(first[0]), jnp.int32(first[1]))
        cur = lax.fori_loop(0, count // unroll, steps(unroll), cur)
        for _ in range(count % unroll // 2):
            cur = steps(2)(0, cur)
        if count % 2:
            update(cur[0], cur[1], sa_scr)

    def below_next(qi, kb):
        last = kb == qi - 1
        return jnp.where(last, qi + 1, qi), jnp.where(last, 0, kb + 1)

    walk(nblk * (nblk - 1) // 2, (1, 0), below_next, False, WALK_UNROLL_BELOW)
    walk(nblk, (0, 0), lambda qi, kb: (qi + 1, kb + 1), True, WALK_UNROLL_DIAG)

    def finish(qi, carry):
        acc = acc_scr[qi]
        o_all = acc[:V_HEAD_DIM] * (1.0 / acc[V_HEAD_DIM:V_HEAD_DIM + 1])
        o = o_all[:, :tq] - lam_ref[:, 0:1] * o_all[:, tq:]
        o = o * lax.rsqrt(jnp.mean(o * o, axis=0, keepdims=True) + SUBLN_EPS)
        o = (o * g_ref[...]) * (1.0 - lam_init)
        rows = pl.ds(pl.multiple_of(qi * tq, tq), tq)
        o_ref[0, 0, rows, :] = (o.T * jax.nn.silu(z_ref[0, 0, rows, :].astype(F32))).astype(BF16)
        return carry

    lax.fori_loop(0, nblk, finish, 0)


def _attn(q, k, v, z2, lam_row, subln_g, lam_init):
    bsz, _, seq, _ = q.shape
    tq = BLK
    nblk = seq // tq
    whole = pl.BlockSpec((1, 1, seq, V_HEAD_DIM), lambda b, h: (b, h, 0, 0))
    return pl.pallas_call(
        functools.partial(_attn_kernel, lam_init=lam_init),
        grid=(bsz, N_HEADS),
        in_specs=[whole, whole, whole, whole, _const_spec((tq, 2 * tq)),
                  _const_spec((1, V_HEAD_DIM)), _const_spec((V_HEAD_DIM, 1))],
        out_specs=whole,
        out_shape=jax.ShapeDtypeStruct(q.shape, BF16),
        scratch_shapes=[
            pltpu.VMEM((nblk, V_HEAD_DIM, 2 * tq), BF16),
            pltpu.VMEM((nblk, V_ROWS, tq), BF16),
            pltpu.VMEM((tq, 2 * tq), F32),
            pltpu.VMEM((tq, 2 * tq), F32),
            pltpu.VMEM((2, 8, 2 * tq), F32),
            pltpu.VMEM((nblk, 8, 2 * tq), F32),
            pltpu.VMEM((nblk, V_ROWS, 2 * tq), F32),
        ],
        compiler_params=_cparams(("parallel", "parallel")),
        name="diff_attn",
    )(q, k, v, z2, _causal_bias(), lam_row, subln_g)


def _final_kernel(o_ref, h_ref, w_ref, g_ref, out_ref):
    y = jnp.concatenate([o_ref[0, hd] for hd in range(N_HEADS)], axis=1)
    h = h_ref[0] + jnp.dot(y, w_ref[...], preferred_element_type=F32)
    out = h * lax.rsqrt(jnp.mean(h * h, axis=-1, keepdims=True) + NORM_EPS) * g_ref[...]
    for j in range(SSM_L):
        out_ref[0, :, j, :] = out[j * BLK_CHUNKS:(j + 1) * BLK_CHUNKS]


def _final(o, h1, out_w, g):
    bsz, seq, _ = h1.shape
    tok = pl.BlockSpec((1, BLK, D_MODEL), lambda b, i: (b, i, 0))
    return pl.pallas_call(
        _final_kernel,
        grid=(bsz, seq // BLK),
        in_specs=[pl.BlockSpec((1, N_HEADS, BLK, V_HEAD_DIM), lambda b, i: (b, 0, i, 0)), tok,
                  _const_spec((D_MODEL, D_MODEL)), _const_spec((1, D_MODEL))],
        out_specs=_NATURAL_TOK,
        out_shape=jax.ShapeDtypeStruct((bsz, seq // SSM_L, SSM_L, D_MODEL), F32),
        compiler_params=_cparams(("parallel", "parallel")),
        name="attn_out_final",
    )(o, h1, out_w, g).reshape(bsz, seq, D_MODEL)


def _rope_tables(seq):
    half = ROT_DIM // 2
    inv = ROPE_THETA ** (-jnp.arange(0, ROT_DIM, 2, dtype=F32) / ROT_DIM)
    row = jnp.arange(seq, dtype=jnp.int32)
    pos = (row // BLK) * BLK + _block_position(row % BLK)
    ang = pos.astype(F32)[:, None] * inv[None, :]
    cos, sin = jnp.cos(ang), jnp.sin(ang)
    ones = jnp.ones((seq, HEAD_DIM - ROT_DIM), F32)
    zeros = jnp.zeros((seq, HEAD_DIM - ROT_DIM), F32)
    zh = jnp.zeros((seq, half), F32)
    cos_h = jnp.concatenate([cos, cos, ones], axis=1)
    lo_h = jnp.concatenate([-sin, zh, zeros], axis=1)
    hi_h = jnp.concatenate([zh, sin, zeros], axis=1)
    rep = LANES // HEAD_DIM
    return jnp.tile(cos_h, (1, rep)), jnp.tile(lo_h, (1, rep)), jnp.tile(hi_h, (1, rep))


def _pick(n, want):
    r = min(n, want)
    assert n % r == 0, (n, want)
    return r


def kernel(x, a_norm_g, a_in_w, a_lambda_re, a_lambda_im, a_log_dt, a_b_re, a_b_im, a_c_re, a_c_im, a_d, a_glu_w, a_glu_b, a_out_w, kv_norm_g, kv_w, b_norm_g, b_in_w, b_lambda_q1, b_lambda_k1, b_lambda_q2, b_lambda_k2, b_subln_g, b_out_w, final_norm_g):
    bsz, seq, d = x.shape
    assert d == D_MODEL and seq % BLK == 0
    nchunk = seq // SSM_L
    x = x.reshape(bsz, nchunk, SSM_L, D_MODEL)

    u, z = _inproj(x, a_norm_g[0][None], a_in_w[0].astype(BF16))
    a_c, c_c, k_c, a_l = _ssm_tables(a_lambda_re[0], a_lambda_im[0], a_log_dt[0], a_b_re[0],
                                     a_b_im[0], a_c_re[0], a_c_im[0], a_d[0])
    y = _ssm(u, a_c, c_c, k_c, a_l, _pick(nchunk, 256))

    cos_t, slo_t, shi_t = _rope_tables(seq)
    h1, q, k, v, z2 = _mid(y, z, x, cos_t, slo_t, shi_t,
                           a_glu_w[0].astype(BF16), a_glu_b[0][None], a_out_w[0].astype(BF16),
                           kv_norm_g[None], kv_w.astype(BF16),
                           b_norm_g[0][None], b_in_w[0].astype(BF16))

    lam_init = 0.8 - 0.6 * math.exp(-0.3 * N_A_LAYERS)
    lam = (jnp.exp(jnp.sum(b_lambda_q1[0] * b_lambda_k1[0]))
           - jnp.exp(jnp.sum(b_lambda_q2[0] * b_lambda_k2[0])) + lam_init)
    lam_row = jnp.full((1, V_HEAD_DIM), lam, F32)
    o = _attn(q, k, v, z2, lam_row, b_subln_g[0][:, None], lam_init)

    return _final(o, h1, b_out_w[0].astype(BF16), final_norm_g[None])
```

```python
import functools
import math

import jax
import jax.numpy as jnp
import numpy as np
from jax import lax
from jax.experimental import pallas as pl
from jax.experimental.pallas import tpu as pltpu

F32 = jnp.float32
BF16 = jnp.bfloat16

D_MODEL = 1024
SSM_GROUPS = 64
SSM_GROUP = 16
SSM_STATE = 64
N_HEADS = 8
HEAD_DIM = 64
V_HEAD_DIM = 128
ROT_DIM = 16
ROPE_THETA = 500000.0
NORM_EPS = 1e-6
SUBLN_EPS = 1e-5
N_A_LAYERS = 1

LANES = 128
SSM_L = 8
GROUPS_PER_TILE = LANES // SSM_GROUP
N_TILES = D_MODEL // LANES
TILE_STATE = GROUPS_PER_TILE * SSM_STATE
BLK = 512
BLK_CHUNKS = BLK // SSM_L
MASK_VALUE = -1e30
Q_SCALE = HEAD_DIM ** -0.5 * math.log2(math.e)
VMEM_LIMIT = 56 * 1024 * 1024


def _cparams(sem):
    return pltpu.CompilerParams(dimension_semantics=sem, vmem_limit_bytes=VMEM_LIMIT)


def _const_spec(shape):
    return pl.BlockSpec(shape, lambda *_: (0,) * len(shape), pipeline_mode=pl.Buffered(1))


def _load_phase_major(x_ref):
    return jnp.concatenate([x_ref[0, :, j, :] for j in range(SSM_L)], axis=0)


_NATURAL_TOK = pl.BlockSpec((1, BLK_CHUNKS, SSM_L, D_MODEL), lambda b, c: (b, c, 0, 0))


def _block_position(row):
    return (row % BLK_CHUNKS) * SSM_L + row // BLK_CHUNKS


def _inproj_kernel(x_ref, g_ref, w_ref, u_ref, z_ref):
    x = _load_phase_major(x_ref)
    xn = x * lax.rsqrt(jnp.mean(x * x, axis=-1, keepdims=True) + NORM_EPS) * g_ref[...]
    uz = jnp.dot(xn.astype(BF16), w_ref[...], preferred_element_type=F32)
    for j in range(SSM_L):
        rows = slice(j * BLK_CHUNKS, (j + 1) * BLK_CHUNKS)
        for kt in range(N_TILES):
            u_ref[0, kt, j] = uz[rows, kt * LANES:(kt + 1) * LANES].astype(BF16)
    z_ref[0] = uz[:, D_MODEL:].astype(BF16)


def _inproj(x, g, w):
    bsz, nchunk, _, _ = x.shape
    seq = nchunk * SSM_L
    tok = pl.BlockSpec((1, BLK, D_MODEL), lambda b, c: (b, c, 0))
    return pl.pallas_call(
        _inproj_kernel,
        grid=(bsz, seq // BLK),
        in_specs=[_NATURAL_TOK, _const_spec((1, D_MODEL)), _const_spec((D_MODEL, 2 * D_MODEL))],
        out_specs=[
            pl.BlockSpec((1, N_TILES, SSM_L, BLK_CHUNKS, LANES), lambda b, c: (b, 0, 0, c, 0)),
            tok,
        ],
        out_shape=[
            jax.ShapeDtypeStruct((bsz, N_TILES, SSM_L, nchunk, LANES), BF16),
            jax.ShapeDtypeStruct((bsz, seq, D_MODEL), BF16),
        ],
        compiler_params=_cparams(("parallel", "parallel")),
        name="s5_inproj",
    )(x, g, w)


def _ssm_tables(lam_re, lam_im, log_dt, b_re, b_im, c_re, c_im, d_skip):
    hi = lax.Precision.HIGHEST
    n_l = SSM_L
    dt = jnp.exp(log_dt)[:, None]
    lr = lam_re * dt
    li = lam_im * dt
    mag = jnp.exp(lr)
    abar_re = mag * jnp.cos(li)
    abar_im = mag * jnp.sin(li)
    den = lam_re * lam_re + lam_im * lam_im
    nr = abar_re - 1.0
    ni = abar_im
    f_re = (nr * lam_re + ni * lam_im) / den
    f_im = (ni * lam_re - nr * lam_im) / den
    bbar_re = f_re[..., None] * b_re - f_im[..., None] * b_im
    bbar_im = f_re[..., None] * b_im + f_im[..., None] * b_re
    n = jnp.arange(n_l + 1, dtype=F32)[:, None, None]
    pmag = jnp.exp(n * lr[None])
    pw_re = pmag * jnp.cos(n * li[None])
    pw_im = pmag * jnp.sin(n * li[None])
    gt, nt = GROUPS_PER_TILE, N_TILES
    bt_re = bbar_re.transpose(0, 2, 1)
    bt_im = bbar_im.transpose(0, 2, 1)
    ab_re = pw_re[:n_l, :, None, :] * bt_re[None] - pw_im[:n_l, :, None, :] * bt_im[None]
    ab_im = pw_re[:n_l, :, None, :] * bt_im[None] + pw_im[:n_l, :, None, :] * bt_re[None]
    kd = (jnp.einsum('gop,tgip->tgio', c_re, ab_re, precision=hi)
          - jnp.einsum('gop,tgip->tgio', c_im, ab_im, precision=hi))
    d_diag = d_skip.reshape(SSM_GROUPS, SSM_GROUP)[:, :, None] * jnp.eye(SSM_GROUP, dtype=F32)
    kd = kd.at[0].add(d_diag)

    def per_tile(a):
        lead, _, r, w = a.shape
        return a.reshape(lead, nt, gt * r, w).transpose(1, 0, 2, 3).reshape(nt, lead * gt * r, w)

    a_c = per_tile(jnp.concatenate([jnp.flip(ab_re, 0), jnp.flip(ab_im, 0)], axis=-1))
    cr = jnp.tile(c_re.transpose(0, 2, 1), (1, 1, n_l))
    ci = jnp.tile(c_im.transpose(0, 2, 1), (1, 1, n_l))
    pr = jnp.repeat(pw_re[1:].transpose(1, 2, 0), SSM_GROUP, axis=-1)
    pi = jnp.repeat(pw_im[1:].transpose(1, 2, 0), SSM_GROUP, axis=-1)
    c_c = per_tile(jnp.stack([cr * pr - ci * pi, -(cr * pi + ci * pr)], axis=0))
    zero = jnp.zeros_like(kd[0])
    k_c = per_tile(jnp.stack(
        [jnp.concatenate([kd[t - j] if j <= t else zero for t in range(n_l)], axis=-1) for j in range(n_l)],
        axis=0))
    a_l = jnp.concatenate([pw_re[n_l].reshape(nt, 1, TILE_STATE),
                           pw_im[n_l].reshape(nt, 1, TILE_STATE)], axis=-1)
    return a_c.astype(BF16), c_c.astype(BF16), k_c.astype(BF16), a_l


def _selection_matrices():
    r = np.arange(2 * SSM_STATE)[:, None]
    c = np.arange(2 * TILE_STATE)[None, :]
    e_in = (r // SSM_STATE == c // TILE_STATE) & (r % SSM_STATE == c % SSM_STATE)
    r = np.arange(SSM_L * SSM_GROUP)[:, None]
    c = np.arange(2 * LANES)[None, :]
    same = ((r // SSM_GROUP) % 2 == c // LANES) & (r % SSM_GROUP == c % SSM_GROUP)
    e_q = np.stack([same & (r // (2 * SSM_GROUP) == q) for q in range(SSM_L // 2)])
    return jnp.asarray(e_in, BF16), jnp.asarray(e_q, BF16)


def _expand_groups(compact, sel, row_group_width, col_group_width):
    full = jnp.dot(compact, sel, preferred_element_type=F32)
    row = lax.broadcasted_iota(jnp.int32, full.shape, 0)
    col = lax.broadcasted_iota(jnp.int32, full.shape, 1)
    gi = (row >> (row_group_width.bit_length() - 1)) & (GROUPS_PER_TILE - 1)
    gj = (col >> (col_group_width.bit_length() - 1)) & (GROUPS_PER_TILE - 1)
    return jnp.where(gi == gj, full, 0.0).astype(BF16)


def _ssm_kernel(u_ref, ac_ref, cc_ref, kc_ref, ein_ref, eq_ref, al_ref, y_ref,
                su_scr, w_scr, carry_scr, win_scr, wst_scr, wu_scr):
    bsz = u_ref.shape[0]
    rows = u_ref.shape[3]
    ts = TILE_STATE

    @pl.when(pl.program_id(1) == 0)
    def _():
        win_scr[...] = _expand_groups(ac_ref[0], ein_ref[...], SSM_GROUP, SSM_STATE)
        for q in range(SSM_L // 2):
            wst_scr[q] = _expand_groups(cc_ref[0], eq_ref[q], SSM_STATE, SSM_GROUP)
            wu_scr[q] = _expand_groups(kc_ref[0], eq_ref[q], SSM_GROUP, SSM_GROUP)
        carry_scr[...] = jnp.zeros_like(carry_scr)

    for b in range(bsz):
        for j in range(SSM_L):
            su_scr[b * rows:(b + 1) * rows, 2 * ts + j * LANES:2 * ts + (j + 1) * LANES] = u_ref[b, 0, j]
    w_scr[...] = jnp.dot(su_scr[:, 2 * ts:], win_scr[...], preferred_element_type=F32)

    a_re = al_ref[0, :, :ts]
    a_im = al_ref[0, :, ts:]

    def body(c, carry):
        out = []
        for b in range(bsz):
            s_re, s_im = carry[2 * b], carry[2 * b + 1]
            row = pl.ds(b * rows + c, 1)
            w_re = w_scr[row, :ts]
            w_im = w_scr[row, ts:]
            w_scr[row, :ts] = s_re
            w_scr[row, ts:] = s_im
            out += [a_re * s_re - a_im * s_im + w_re, a_re * s_im + a_im * s_re + w_im]
        return tuple(out)

    init = []
    for b in range(bsz):
        init += [carry_scr[8 * b:8 * b + 1, :ts], carry_scr[8 * b:8 * b + 1, ts:]]
    final = lax.fori_loop(0, rows, body, tuple(init))
    for b in range(bsz):
        carry_scr[8 * b:8 * b + 1, :ts] = final[2 * b]
        carry_scr[8 * b:8 * b + 1, ts:] = final[2 * b + 1]

    su_scr[:, :2 * ts] = w_scr[...].astype(BF16)
    for q in range(SSM_L // 2):
        k_used = (2 * q + 2) * LANES
        yy = (jnp.dot(su_scr[:, :2 * ts], wst_scr[q], preferred_element_type=F32)
              + jnp.dot(su_scr[:, 2 * ts:2 * ts + k_used], wu_scr[q, :k_used, :],
                        preferred_element_type=F32))
        for b in range(bsz):
            y_ref[b, 0, 2 * q] = yy[b * rows:(b + 1) * rows, :LANES].astype(BF16)
            y_ref[b, 0, 2 * q + 1] = yy[b * rows:(b + 1) * rows, LANES:].astype(BF16)


def _ssm(u, a_c, c_c, k_c, a_l, rows):
    bsz, _, _, nchunk, _ = u.shape
    k_all = 2 * TILE_STATE + SSM_L * LANES
    blk = pl.BlockSpec((bsz, 1, SSM_L, rows, LANES), lambda t, i: (0, t, 0, i, 0))
    return pl.pallas_call(
        _ssm_kernel,
        grid=(N_TILES, nchunk // rows),
        in_specs=[
            blk,
            pl.BlockSpec((1, SSM_L * LANES, 2 * SSM_STATE), lambda t, i: (t, 0, 0)),
            pl.BlockSpec((1, 2 * TILE_STATE, SSM_L * SSM_GROUP), lambda t, i: (t, 0, 0)),
            pl.BlockSpec((1, SSM_L * LANES, SSM_L * SSM_GROUP), lambda t, i: (t, 0, 0)),
            _const_spec((2 * SSM_STATE, 2 * TILE_STATE)),
            _const_spec((SSM_L // 2, SSM_L * SSM_GROUP, 2 * LANES)),
            pl.BlockSpec((1, 1, 2 * TILE_STATE), lambda t, i: (t, 0, 0)),
        ],
        out_specs=blk,
        out_shape=jax.ShapeDtypeStruct(u.shape, BF16),
        scratch_shapes=[
            pltpu.VMEM((bsz * rows, k_all), BF16),
            pltpu.VMEM((bsz * rows, 2 * TILE_STATE), F32),
            pltpu.VMEM((8 * bsz, 2 * TILE_STATE), F32),
            pltpu.VMEM((SSM_L * LANES, 2 * TILE_STATE), BF16),
            pltpu.VMEM((SSM_L // 2, 2 * TILE_STATE, 2 * LANES), BF16),
            pltpu.VMEM((SSM_L // 2, SSM_L * LANES, 2 * LANES), BF16),
        ],
        compiler_params=_cparams(("parallel", "arbitrary")),
        name="s5_scan",
    )(u, a_c, c_c, k_c, *_selection_matrices(), a_l)


def _rope(t, cos_f, sin_lo, sin_hi):
    half = ROT_DIM // 2
    width = t.shape[1]
    return (t * cos_f + pltpu.roll(t, half, 1) * sin_hi
            + pltpu.roll(t, width - half, 1) * sin_lo)


def _mid_kernel(y_ref, z_ref, x_ref, cos_ref, slo_ref, shi_ref, gluw_ref, glub_ref, outw_ref,
                kvg_ref, kvw_ref, bg_ref, binw_ref,
                h_ref, q_ref, k_ref, v_ref, z2_ref):
    y = jnp.concatenate(
        [jnp.concatenate([y_ref[0, kt, j] for kt in range(N_TILES)], axis=1) for j in range(SSM_L)],
        axis=0).astype(F32)
    y = 0.5 * y * (1.0 + lax.erf(y * (2.0 ** -0.5)))
    gate = jnp.dot(y.astype(BF16), gluw_ref[...], preferred_element_type=F32) + glub_ref[...]
    y = y * jax.nn.sigmoid(gate)
    y = y * jax.nn.silu(z_ref[0].astype(F32))
    h = _load_phase_major(x_ref) + jnp.dot(y.astype(BF16), outw_ref[...], preferred_element_type=F32)
    h_ref[0] = h

    hn = h * lax.rsqrt(jnp.mean(h * h, axis=-1, keepdims=True) + NORM_EPS)
    reps = D_MODEL // LANES
    cos_f = jnp.concatenate([cos_ref[...]] * reps, axis=1)
    sin_lo = jnp.concatenate([slo_ref[...]] * reps, axis=1)
    sin_hi = jnp.concatenate([shi_ref[...]] * reps, axis=1)

    kv = jnp.dot((hn * kvg_ref[...]).astype(BF16), kvw_ref[...], preferred_element_type=F32)
    k = _rope(kv[:, :D_MODEL], cos_f, sin_lo, sin_hi).astype(BF16)
    v = kv[:, D_MODEL:].astype(BF16)
    qz = jnp.dot((hn * bg_ref[...]).astype(BF16), binw_ref[...], preferred_element_type=F32)
    q = (_rope(qz[:, :D_MODEL], cos_f, sin_lo, sin_hi) * Q_SCALE).astype(BF16)
    z2 = qz[:, D_MODEL:].astype(BF16)
    for hd in range(N_HEADS):
        cols = slice(hd * V_HEAD_DIM, (hd + 1) * V_HEAD_DIM)
        q_ref[0, hd] = q[:, cols]
        k_ref[0, hd] = k[:, cols]
        v_ref[0, hd] = v[:, cols]
        z2_ref[0, hd] = z2[:, cols]


def _mid(y, z, x, cos_t, slo_t, shi_t, glu_w, glu_b, out_w, kv_g, kv_w, b_g, bin_w):
    bsz, nchunk, _, _ = x.shape
    seq = nchunk * SSM_L
    tok = pl.BlockSpec((1, BLK, D_MODEL), lambda b, c: (b, c, 0))
    tab = pl.BlockSpec((BLK, LANES), lambda b, c: (c, 0))
    heads = pl.BlockSpec((1, N_HEADS, BLK, V_HEAD_DIM), lambda b, c: (b, 0, c, 0))
    per_head = jax.ShapeDtypeStruct((bsz, N_HEADS, seq, V_HEAD_DIM), BF16)
    return pl.pallas_call(
        _mid_kernel,
        grid=(bsz, seq // BLK),
        in_specs=[
            pl.BlockSpec((1, N_TILES, SSM_L, BLK_CHUNKS, LANES), lambda b, c: (b, 0, 0, c, 0)),
            tok, _NATURAL_TOK, tab, tab, tab,
            _const_spec((D_MODEL, D_MODEL)), _const_spec((1, D_MODEL)), _const_spec((D_MODEL, D_MODEL)),
            _const_spec((1, D_MODEL)), _const_spec((D_MODEL, 2 * D_MODEL)),
            _const_spec((1, D_MODEL)), _const_spec((D_MODEL, 2 * D_MODEL)),
        ],
        out_specs=[tok, heads, heads, heads, heads],
        out_shape=[jax.ShapeDtypeStruct((bsz, seq, D_MODEL), F32), per_head, per_head, per_head, per_head],
        compiler_params=_cparams(("parallel", "parallel")),
        name="s5_out_qkv",
    )(y, z, x, cos_t, slo_t, shi_t, glu_w, glu_b, out_w, kv_g, kv_w, b_g, bin_w)


V_ROWS = V_HEAD_DIM + 16
WALK_UNROLL_BELOW = 24
WALK_UNROLL_DIAG = 16


def _causal_bias():
    pos = _block_position(np.arange(BLK))
    visible = pos[:, None] <= np.concatenate([pos, pos])[None, :]
    return jnp.asarray(np.where(visible, 0.0, MASK_VALUE), F32)


def _attn_kernel(q_ref, k_ref, v_ref, z_ref, bias_ref, lam_ref, g_ref, o_ref,
                 qs_scr, vt_scr, sa_scr, sb_scr, bm_scr, m_scr, acc_scr, *, lam_init):
    tq = BLK
    nblk = k_ref.shape[2] // tq

    for c in range(nblk):
        rows = slice(c * tq, (c + 1) * tq)
        vt_scr[c, :V_HEAD_DIM, :] = v_ref[0, 0, rows, :].astype(F32).T.astype(BF16)
        one_row = lax.broadcasted_iota(jnp.int32, (V_ROWS - V_HEAD_DIM, tq), 0) == 0
        vt_scr[c, V_HEAD_DIM:, :] = jnp.where(one_row, 1.0, 0.0).astype(BF16)
        qt = q_ref[0, 0, rows, :].astype(F32).T
        dim = lax.broadcasted_iota(jnp.int32, qt.shape, 0)
        qs_scr[c, :, :tq] = jnp.where(dim < HEAD_DIM, qt, 0.0).astype(BF16)
        qs_scr[c, :, tq:] = jnp.where(dim >= HEAD_DIM, qt, 0.0).astype(BF16)
    m_scr[...] = jnp.full_like(m_scr, MASK_VALUE)
    acc_scr[...] = jnp.zeros_like(acc_scr)

    def scores(qi, kb, s_scr, diagonal):
        start = pl.multiple_of(kb * tq, tq)
        k = k_ref[0, 0, pl.ds(start, tq), :]
        s = jnp.dot(k, qs_scr[qi], preferred_element_type=F32)
        s = s + bias_ref[...] if diagonal else s
        s_scr[...] = s
        bm_scr[0 if s_scr is sa_scr else 1, 0:1, :] = jnp.max(s, axis=0, keepdims=True)

    def update(qi, kb, s_scr):
        s = s_scr[...]
        m_prev = m_scr[qi, 0:1, :]
        m_new = jnp.maximum(m_prev, bm_scr[0 if s_scr is sa_scr else 1, 0:1, :])
        alpha = jnp.exp2(m_prev - m_new)
        p = jnp.exp2(s - m_new).astype(BF16)
        pv = jnp.dot(vt_scr[kb], p, preferred_element_type=F32)
        acc_scr[qi] = alpha * acc_scr[qi] + pv
        m_scr[qi, 0:1, :] = m_new

    def finish(qi):
        acc = acc_scr[qi]
        o_all = acc[:V_HEAD_DIM] * (1.0 / acc[V_HEAD_DIM:V_HEAD_DIM + 1])
        o = o_all[:, :tq] - lam_ref[:, 0:1] * o_all[:, tq:]
        o = o * lax.rsqrt(jnp.mean(o * o, axis=0, keepdims=True) + SUBLN_EPS)
        o = (o * g_ref[...]) * (1.0 - lam_init)
        rows = pl.ds(pl.multiple_of(qi * tq, tq), tq)
        o_ref[0, 0, rows, :] = (o.T * jax.nn.silu(z_ref[0, 0, rows, :].astype(F32))).astype(BF16)

    def walk(count, first, following, diagonal, unroll):
        if count == 0:
            return
        last_blk = nblk - 1
        scores(first[0], first[1], sa_scr, diagonal)

        def steps(n):
            def body(_, cur):
                for step in range(n):
                    here, there = (sa_scr, sb_scr) if step % 2 == 0 else (sb_scr, sa_scr)
                    nxt = following(*cur)
                    scores(jnp.minimum(nxt[0], last_blk), jnp.minimum(nxt[1], last_blk), there, diagonal)
                    update(cur[0], cur[1], here)
                    if diagonal:
                        finish(cur[0])
                    cur = nxt
                return cur
            return body

        cur = (jnp.int32(first[0]), jnp.int32(first[1]))
        cur = lax.fori_loop(0, count // unroll, steps(unroll), cur)
        for _ in range(count % unroll // 2):
            cur = steps(2)(0, cur)
        if count % 2:
            update(cur[0], cur[1], sa_scr)
            if diagonal:
                finish(cur[0])

    def below_next(qi, kb):
        last = kb == qi - 1
        return jnp.where(last, qi + 1, qi), jnp.where(last, 0, kb + 1)

    walk(nblk * (nblk - 1) // 2, (1, 0), below_next, False, WALK_UNROLL_BELOW)
    walk(nblk, (0, 0), lambda qi, kb: (qi + 1, kb + 1), True, WALK_UNROLL_DIAG)


def _attn(q, k, v, z2, lam_row, subln_g, lam_init):
    bsz, _, seq, _ = q.shape
    tq = BLK
    nblk = seq // tq
    whole = pl.BlockSpec((1, 1, seq, V_HEAD_DIM), lambda b, h: (b, h, 0, 0))
    return pl.pallas_call(
        functools.partial(_attn_kernel, lam_init=lam_init),
        grid=(bsz, N_HEADS),
        in_specs=[whole, whole, whole, whole, _const_spec((tq, 2 * tq)),
                  _const_spec((1, V_HEAD_DIM)), _const_spec((V_HEAD_DIM, 1))],
        out_specs=whole,
        out_shape=jax.ShapeDtypeStruct(q.shape, BF16),
        scratch_shapes=[
            pltpu.VMEM((nblk, V_HEAD_DIM, 2 * tq), BF16),
            pltpu.VMEM((nblk, V_ROWS, tq), BF16),
            pltpu.VMEM((tq, 2 * tq), F32),
            pltpu.VMEM((tq, 2 * tq), F32),
            pltpu.VMEM((2, 8, 2 * tq), F32),
            pltpu.VMEM((nblk, 8, 2 * tq), F32),
            pltpu.VMEM((nblk, V_ROWS, 2 * tq), F32),
        ],
        compiler_params=_cparams(("parallel", "parallel")),
        name="diff_attn",
    )(q, k, v, z2, _causal_bias(), lam_row, subln_g)


def _final_kernel(o_ref, h_ref, w_ref, g_ref, out_ref):
    y = jnp.concatenate([o_ref[0, hd] for hd in range(N_HEADS)], axis=1)
    h = h_ref[0] + jnp.dot(y, w_ref[...], preferred_element_type=F32)
    out = h * lax.rsqrt(jnp.mean(h * h, axis=-1, keepdims=True) + NORM_EPS) * g_ref[...]
    for j in range(SSM_L):
        out_ref[0, :, j, :] = out[j * BLK_CHUNKS:(j + 1) * BLK_CHUNKS]


def _final(o, h1, out_w, g):
    bsz, seq, _ = h1.shape
    tok = pl.BlockSpec((1, BLK, D_MODEL), lambda b, i: (b, i, 0))
    return pl.pallas_call(
        _final_kernel,
        grid=(bsz, seq // BLK),
        in_specs=[pl.BlockSpec((1, N_HEADS, BLK, V_HEAD_DIM), lambda b, i: (b, 0, i, 0)), tok,
                  _const_spec((D_MODEL, D_MODEL)), _const_spec((1, D_MODEL))],
        out_specs=_NATURAL_TOK,
        out_shape=jax.ShapeDtypeStruct((bsz, seq // SSM_L, SSM_L, D_MODEL), F32),
        compiler_params=_cparams(("parallel", "parallel")),
        name="attn_out_final",
    )(o, h1, out_w, g).reshape(bsz, seq, D_MODEL)


def _rope_tables(seq):
    half = ROT_DIM // 2
    inv = ROPE_THETA ** (-jnp.arange(0, ROT_DIM, 2, dtype=F32) / ROT_DIM)
    row = jnp.arange(seq, dtype=jnp.int32)
    pos = (row // BLK) * BLK + _block_position(row % BLK)
    ang = pos.astype(F32)[:, None] * inv[None, :]
    cos, sin = jnp.cos(ang), jnp.sin(ang)
    ones = jnp.ones((seq, HEAD_DIM - ROT_DIM), F32)
    zeros = jnp.zeros((seq, HEAD_DIM - ROT_DIM), F32)
    zh = jnp.zeros((seq, half), F32)
    cos_h = jnp.concatenate([cos, cos, ones], axis=1)
    lo_h = jnp.concatenate([-sin, zh, zeros], axis=1)
    hi_h = jnp.concatenate([zh, sin, zeros], axis=1)
    rep = LANES // HEAD_DIM
    return jnp.tile(cos_h, (1, rep)), jnp.tile(lo_h, (1, rep)), jnp.tile(hi_h, (1, rep))


def _pick(n, want):
    r = min(n, want)
    assert n % r == 0, (n, want)
    return r


def kernel(x, a_norm_g, a_in_w, a_lambda_re, a_lambda_im, a_log_dt, a_b_re, a_b_im, a_c_re, a_c_im, a_d, a_glu_w, a_glu_b, a_out_w, kv_norm_g, kv_w, b_norm_g, b_in_w, b_lambda_q1, b_lambda_k1, b_lambda_q2, b_lambda_k2, b_subln_g, b_out_w, final_norm_g):
    bsz, seq, d = x.shape
    assert d == D_MODEL and seq % BLK == 0
    nchunk = seq // SSM_L
    x = x.reshape(bsz, nchunk, SSM_L, D_MODEL)

    u, z = _inproj(x, a_norm_g[0][None], a_in_w[0].astype(BF16))
    a_c, c_c, k_c, a_l = _ssm_tables(a_lambda_re[0], a_lambda_im[0], a_log_dt[0], a_b_re[0],
                                     a_b_im[0], a_c_re[0], a_c_im[0], a_d[0])
    y = _ssm(u, a_c, c_c, k_c, a_l, _pick(nchunk, 256))

    cos_t, slo_t, shi_t = _rope_tables(seq)
    h1, q, k, v, z2 = _mid(y, z, x, cos_t, slo_t, shi_t,
                           a_glu_w[0].astype(BF16), a_glu_b[0][None], a_out_w[0].astype(BF16),
                           kv_norm_g[None], kv_w.astype(BF16),
                           b_norm_g[0][None], b_in_w[0].astype(BF16))

    lam_init = 0.8 - 0.6 * math.exp(-0.3 * N_A_LAYERS)
    lam = (jnp.exp(jnp.sum(b_lambda_q1[0] * b_lambda_k1[0]))
           - jnp.exp(jnp.sum(b_lambda_q2[0] * b_lambda_k2[0])) + lam_init)
    lam_row = jnp.full((1, V_HEAD_DIM), lam, F32)
    o = _attn(q, k, v, z2, lam_row, b_subln_g[0][:, None], lam_init)

    return _final(o, h1, b_out_w[0].astype(BF16), final_norm_g[None])
```

```python
import functools
import math

import jax
import jax.numpy as jnp
import numpy as np
from jax import lax
from jax.experimental import pallas as pl
from jax.experimental.pallas import tpu as pltpu

F32 = jnp.float32
BF16 = jnp.bfloat16

D_MODEL = 1024
SSM_GROUPS = 64
SSM_GROUP = 16
SSM_STATE = 64
N_HEADS = 8
HEAD_DIM = 64
V_HEAD_DIM = 128
ROT_DIM = 16
ROPE_THETA = 500000.0
NORM_EPS = 1e-6
SUBLN_EPS = 1e-5
N_A_LAYERS = 1

LANES = 128
SSM_L = 8
GROUPS_PER_TILE = LANES // SSM_GROUP
N_TILES = D_MODEL // LANES
TILE_STATE = GROUPS_PER_TILE * SSM_STATE
BLK = 512
BLK_CHUNKS = BLK // SSM_L
SCAN_ROWS = 512
MASK_VALUE = -1e30
Q_SCALE = HEAD_DIM ** -0.5 * math.log2(math.e)
VMEM_LIMIT = 56 * 1024 * 1024


def _cparams(sem):
    return pltpu.CompilerParams(dimension_semantics=sem, vmem_limit_bytes=VMEM_LIMIT)


def _const_spec(shape):
    return pl.BlockSpec(shape, lambda *_: (0,) * len(shape), pipeline_mode=pl.Buffered(1))


def _load_phase_major(x_ref):
    return jnp.concatenate([x_ref[0, :, j, :] for j in range(SSM_L)], axis=0)


_NATURAL_TOK = pl.BlockSpec((1, BLK_CHUNKS, SSM_L, D_MODEL), lambda b, c: (b, c, 0, 0))


def _block_position(row):
    return (row % BLK_CHUNKS) * SSM_L + row // BLK_CHUNKS


def _inproj_kernel(x_ref, g_ref, w_ref, u_ref, z_ref):
    x = _load_phase_major(x_ref)
    xn = x * lax.rsqrt(jnp.mean(x * x, axis=-1, keepdims=True) + NORM_EPS) * g_ref[...]
    uz = jnp.dot(xn.astype(BF16), w_ref[...], preferred_element_type=F32)
    for j in range(SSM_L):
        rows = slice(j * BLK_CHUNKS, (j + 1) * BLK_CHUNKS)
        for kt in range(N_TILES):
            u_ref[0, kt, j] = uz[rows, kt * LANES:(kt + 1) * LANES].astype(BF16)
    z_ref[0] = uz[:, D_MODEL:].astype(BF16)


def _inproj(x, g, w):
    bsz, nchunk, _, _ = x.shape
    seq = nchunk * SSM_L
    tok = pl.BlockSpec((1, BLK, D_MODEL), lambda b, c: (b, c, 0))
    return pl.pallas_call(
        _inproj_kernel,
        grid=(bsz, seq // BLK),
        in_specs=[_NATURAL_TOK, _const_spec((1, D_MODEL)), _const_spec((D_MODEL, 2 * D_MODEL))],
        out_specs=[
            pl.BlockSpec((1, N_TILES, SSM_L, BLK_CHUNKS, LANES), lambda b, c: (b, 0, 0, c, 0)),
            tok,
        ],
        out_shape=[
            jax.ShapeDtypeStruct((bsz, N_TILES, SSM_L, nchunk, LANES), BF16),
            jax.ShapeDtypeStruct((bsz, seq, D_MODEL), BF16),
        ],
        compiler_params=_cparams(("parallel", "parallel")),
        name="s5_inproj",
    )(x, g, w)


def _ssm_tables(lam_re, lam_im, log_dt, b_re, b_im, c_re, c_im, d_skip):
    hi = lax.Precision.HIGHEST
    n_l = SSM_L
    dt = jnp.exp(log_dt)[:, None]
    lr = lam_re * dt
    li = lam_im * dt
    mag = jnp.exp(lr)
    abar_re = mag * jnp.cos(li)
    abar_im = mag * jnp.sin(li)
    den = lam_re * lam_re + lam_im * lam_im
    nr = abar_re - 1.0
    ni = abar_im
    f_re = (nr * lam_re + ni * lam_im) / den
    f_im = (ni * lam_re - nr * lam_im) / den
    bbar_re = f_re[..., None] * b_re - f_im[..., None] * b_im
    bbar_im = f_re[..., None] * b_im + f_im[..., None] * b_re
    n = jnp.arange(n_l + 1, dtype=F32)[:, None, None]
    pmag = jnp.exp(n * lr[None])
    pw_re = pmag * jnp.cos(n * li[None])
    pw_im = pmag * jnp.sin(n * li[None])
    gt, nt = GROUPS_PER_TILE, N_TILES
    bt_re = bbar_re.transpose(0, 2, 1)
    bt_im = bbar_im.transpose(0, 2, 1)
    ab_re = pw_re[:n_l, :, None, :] * bt_re[None] - pw_im[:n_l, :, None, :] * bt_im[None]
    ab_im = pw_re[:n_l, :, None, :] * bt_im[None] + pw_im[:n_l, :, None, :] * bt_re[None]
    kd = (jnp.einsum('gop,tgip->tgio', c_re, ab_re, precision=hi)
          - jnp.einsum('gop,tgip->tgio', c_im, ab_im, precision=hi))
    d_diag = d_skip.reshape(SSM_GROUPS, SSM_GROUP)[:, :, None] * jnp.eye(SSM_GROUP, dtype=F32)
    kd = kd.at[0].add(d_diag)

    def per_tile(a):
        lead, _, r, w = a.shape
        return a.reshape(lead, nt, gt * r, w).transpose(1, 0, 2, 3).reshape(nt, lead * gt * r, w)

    a_c = per_tile(jnp.concatenate([jnp.flip(ab_re, 0), jnp.flip(ab_im, 0)], axis=-1))
    cr = jnp.tile(c_re.transpose(0, 2, 1), (1, 1, n_l))
    ci = jnp.tile(c_im.transpose(0, 2, 1), (1, 1, n_l))
    pr = jnp.repeat(pw_re[1:].transpose(1, 2, 0), SSM_GROUP, axis=-1)
    pi = jnp.repeat(pw_im[1:].transpose(1, 2, 0), SSM_GROUP, axis=-1)
    c_c = per_tile(jnp.stack([cr * pr - ci * pi, -(cr * pi + ci * pr)], axis=0))
    zero = jnp.zeros_like(kd[0])
    k_c = per_tile(jnp.stack(
        [jnp.concatenate([kd[t - j] if j <= t else zero for t in range(n_l)], axis=-1) for j in range(n_l)],
        axis=0))
    a_l = jnp.concatenate([pw_re[n_l].reshape(nt, 1, TILE_STATE),
                           pw_im[n_l].reshape(nt, 1, TILE_STATE)], axis=-1)
    return a_c.astype(BF16), c_c.astype(BF16), k_c.astype(BF16), a_l


def _selection_matrices():
    r = np.arange(2 * SSM_STATE)[:, None]
    c = np.arange(2 * TILE_STATE)[None, :]
    e_in = (r // SSM_STATE == c // TILE_STATE) & (r % SSM_STATE == c % SSM_STATE)
    r = np.arange(SSM_L * SSM_GROUP)[:, None]
    c = np.arange(2 * LANES)[None, :]
    same = ((r // SSM_GROUP) % 2 == c // LANES) & (r % SSM_GROUP == c % SSM_GROUP)
    e_q = np.stack([same & (r // (2 * SSM_GROUP) == q) for q in range(SSM_L // 2)])
    return jnp.asarray(e_in, BF16), jnp.asarray(e_q, BF16)


def _expand_groups(compact, sel, row_group_width, col_group_width):
    full = jnp.dot(compact, sel, preferred_element_type=F32)
    row = lax.broadcasted_iota(jnp.int32, full.shape, 0)
    col = lax.broadcasted_iota(jnp.int32, full.shape, 1)
    gi = (row >> (row_group_width.bit_length() - 1)) & (GROUPS_PER_TILE - 1)
    gj = (col >> (col_group_width.bit_length() - 1)) & (GROUPS_PER_TILE - 1)
    return jnp.where(gi == gj, full, 0.0).astype(BF16)


def _ssm_kernel(u_ref, ac_ref, cc_ref, kc_ref, ein_ref, eq_ref, al_ref, y_ref,
                su_scr, w_scr, carry_scr, win_scr, wst_scr, wu_scr):
    bsz = u_ref.shape[0]
    rows = u_ref.shape[3]
    ts = TILE_STATE

    @pl.when(pl.program_id(1) == 0)
    def _():
        win_scr[...] = _expand_groups(ac_ref[0], ein_ref[...], SSM_GROUP, SSM_STATE)
        for q in range(SSM_L // 2):
            wst_scr[q] = _expand_groups(cc_ref[0], eq_ref[q], SSM_STATE, SSM_GROUP)
            wu_scr[q] = _expand_groups(kc_ref[0], eq_ref[q], SSM_GROUP, SSM_GROUP)
        carry_scr[...] = jnp.zeros_like(carry_scr)

    for b in range(bsz):
        for j in range(SSM_L):
            su_scr[b * rows:(b + 1) * rows, 2 * ts + j * LANES:2 * ts + (j + 1) * LANES] = u_ref[b, 0, j]
    w_scr[...] = jnp.dot(su_scr[:, 2 * ts:], win_scr[...], preferred_element_type=F32)

    a_re = al_ref[0, :, :ts]
    a_im = al_ref[0, :, ts:]

    def body(c, carry):
        out = []
        for b in range(bsz):
            s_re, s_im = carry[2 * b], carry[2 * b + 1]
            row = pl.ds(b * rows + c, 1)
            w_re = w_scr[row, :ts]
            w_im = w_scr[row, ts:]
            w_scr[row, :ts] = s_re
            w_scr[row, ts:] = s_im
            out += [a_re * s_re - a_im * s_im + w_re, a_re * s_im + a_im * s_re + w_im]
        return tuple(out)

    init = []
    for b in range(bsz):
        init += [carry_scr[8 * b:8 * b + 1, :ts], carry_scr[8 * b:8 * b + 1, ts:]]
    final = lax.fori_loop(0, rows, body, tuple(init))
    for b in range(bsz):
        carry_scr[8 * b:8 * b + 1, :ts] = final[2 * b]
        carry_scr[8 * b:8 * b + 1, ts:] = final[2 * b + 1]

    su_scr[:, :2 * ts] = w_scr[...].astype(BF16)
    for q in range(SSM_L // 2):
        k_used = (2 * q + 2) * LANES
        yy = (jnp.dot(su_scr[:, :2 * ts], wst_scr[q], preferred_element_type=F32)
              + jnp.dot(su_scr[:, 2 * ts:2 * ts + k_used], wu_scr[q, :k_used, :],
                        preferred_element_type=F32))
        for b in range(bsz):
            y_ref[b, 0, 2 * q] = yy[b * rows:(b + 1) * rows, :LANES].astype(BF16)
            y_ref[b, 0, 2 * q + 1] = yy[b * rows:(b + 1) * rows, LANES:].astype(BF16)


def _ssm(u, a_c, c_c, k_c, a_l, rows):
    bsz, _, _, nchunk, _ = u.shape
    k_all = 2 * TILE_STATE + SSM_L * LANES
    blk = pl.BlockSpec((bsz, 1, SSM_L, rows, LANES), lambda t, i: (0, t, 0, i, 0))
    return pl.pallas_call(
        _ssm_kernel,
        grid=(N_TILES, nchunk // rows),
        in_specs=[
            blk,
            pl.BlockSpec((1, SSM_L * LANES, 2 * SSM_STATE), lambda t, i: (t, 0, 0)),
            pl.BlockSpec((1, 2 * TILE_STATE, SSM_L * SSM_GROUP), lambda t, i: (t, 0, 0)),
            pl.BlockSpec((1, SSM_L * LANES, SSM_L * SSM_GROUP), lambda t, i: (t, 0, 0)),
            _const_spec((2 * SSM_STATE, 2 * TILE_STATE)),
            _const_spec((SSM_L // 2, SSM_L * SSM_GROUP, 2 * LANES)),
            pl.BlockSpec((1, 1, 2 * TILE_STATE), lambda t, i: (t, 0, 0)),
        ],
        out_specs=blk,
        out_shape=jax.ShapeDtypeStruct(u.shape, BF16),
        scratch_shapes=[
            pltpu.VMEM((bsz * rows, k_all), BF16),
            pltpu.VMEM((bsz * rows, 2 * TILE_STATE), F32),
            pltpu.VMEM((8 * bsz, 2 * TILE_STATE), F32),
            pltpu.VMEM((SSM_L * LANES, 2 * TILE_STATE), BF16),
            pltpu.VMEM((SSM_L // 2, 2 * TILE_STATE, 2 * LANES), BF16),
            pltpu.VMEM((SSM_L // 2, SSM_L * LANES, 2 * LANES), BF16),
        ],
        compiler_params=_cparams(("parallel", "arbitrary")),
        name="s5_scan",
    )(u, a_c, c_c, k_c, *_selection_matrices(), a_l)


def _rope(t, cos_f, sin_lo, sin_hi):
    half = ROT_DIM // 2
    width = t.shape[1]
    return (t * cos_f + pltpu.roll(t, half, 1) * sin_hi
            + pltpu.roll(t, width - half, 1) * sin_lo)


def _mid_kernel(y_ref, z_ref, x_ref, cos_ref, slo_ref, shi_ref, gluw_ref, glub_ref, outw_ref,
                kvg_ref, kvw_ref, bg_ref, binw_ref,
                h_ref, q_ref, k_ref, v_ref, z2_ref):
    y = jnp.concatenate(
        [jnp.concatenate([y_ref[0, kt, j] for kt in range(N_TILES)], axis=1) for j in range(SSM_L)],
        axis=0).astype(F32)
    y = 0.5 * y * (1.0 + lax.erf(y * (2.0 ** -0.5)))
    gate = jnp.dot(y.astype(BF16), gluw_ref[...], preferred_element_type=F32) + glub_ref[...]
    y = y * jax.nn.sigmoid(gate)
    y = y * jax.nn.silu(z_ref[0].astype(F32))
    h = _load_phase_major(x_ref) + jnp.dot(y.astype(BF16), outw_ref[...], preferred_element_type=F32)
    h_ref[0] = h

    hn = h * lax.rsqrt(jnp.mean(h * h, axis=-1, keepdims=True) + NORM_EPS)
    reps = D_MODEL // LANES
    cos_f = jnp.concatenate([cos_ref[...]] * reps, axis=1)
    sin_lo = jnp.concatenate([slo_ref[...]] * reps, axis=1)
    sin_hi = jnp.concatenate([shi_ref[...]] * reps, axis=1)

    kv = jnp.dot((hn * kvg_ref[...]).astype(BF16), kvw_ref[...], preferred_element_type=F32)
    k = _rope(kv[:, :D_MODEL], cos_f, sin_lo, sin_hi).astype(BF16)
    v = kv[:, D_MODEL:].astype(BF16)
    qz = jnp.dot((hn * bg_ref[...]).astype(BF16), binw_ref[...], preferred_element_type=F32)
    q = (_rope(qz[:, :D_MODEL], cos_f, sin_lo, sin_hi) * Q_SCALE).astype(BF16)
    z2 = qz[:, D_MODEL:].astype(BF16)
    for hd in range(N_HEADS):
        cols = slice(hd * V_HEAD_DIM, (hd + 1) * V_HEAD_DIM)
        q_ref[0, hd] = q[:, cols]
        k_ref[0, hd] = k[:, cols]
        v_ref[0, hd] = v[:, cols]
        z2_ref[0, hd] = z2[:, cols]


def _mid(y, z, x, cos_t, slo_t, shi_t, glu_w, glu_b, out_w, kv_g, kv_w, b_g, bin_w):
    bsz, nchunk, _, _ = x.shape
    seq = nchunk * SSM_L
    tok = pl.BlockSpec((1, BLK, D_MODEL), lambda b, c: (b, c, 0))
    tab = pl.BlockSpec((BLK, LANES), lambda b, c: (c, 0))
    heads = pl.BlockSpec((1, N_HEADS, BLK, V_HEAD_DIM), lambda b, c: (b, 0, c, 0))
    per_head = jax.ShapeDtypeStruct((bsz, N_HEADS, seq, V_HEAD_DIM), BF16)
    return pl.pallas_call(
        _mid_kernel,
        grid=(bsz, seq // BLK),
        in_specs=[
            pl.BlockSpec((1, N_TILES, SSM_L, BLK_CHUNKS, LANES), lambda b, c: (b, 0, 0, c, 0)),
            tok, _NATURAL_TOK, tab, tab, tab,
            _const_spec((D_MODEL, D_MODEL)), _const_spec((1, D_MODEL)), _const_spec((D_MODEL, D_MODEL)),
            _const_spec((1, D_MODEL)), _const_spec((D_MODEL, 2 * D_MODEL)),
            _const_spec((1, D_MODEL)), _const_spec((D_MODEL, 2 * D_MODEL)),
        ],
        out_specs=[tok, heads, heads, heads, heads],
        out_shape=[jax.ShapeDtypeStruct((bsz, seq, D_MODEL), F32), per_head, per_head, per_head, per_head],
        compiler_params=_cparams(("parallel", "parallel")),
        name="s5_out_qkv",
    )(y, z, x, cos_t, slo_t, shi_t, glu_w, glu_b, out_w, kv_g, kv_w, b_g, bin_w)


V_ROWS = V_HEAD_DIM + 16
WALK_UNROLL_BELOW = 24
WALK_UNROLL_DIAG = 16


def _causal_bias():
    pos = _block_position(np.arange(BLK))
    visible = pos[:, None] <= np.concatenate([pos, pos])[None, :]
    return jnp.asarray(np.where(visible, 0.0, MASK_VALUE), F32)


def _attn_kernel(q_ref, k_ref, v_ref, z_ref, bias_ref, lam_ref, g_ref, o_ref,
                 qs_scr, vt_scr, sa_scr, sb_scr, bm_scr, m_scr, acc_scr, *, lam_init):
    tq = BLK
    nblk = k_ref.shape[2] // tq

    for c in range(nblk):
        rows = slice(c * tq, (c + 1) * tq)
        vt_scr[c, :V_HEAD_DIM, :] = v_ref[0, 0, rows, :].astype(F32).T.astype(BF16)
        one_row = lax.broadcasted_iota(jnp.int32, (V_ROWS - V_HEAD_DIM, tq), 0) == 0
        vt_scr[c, V_HEAD_DIM:, :] = jnp.where(one_row, 1.0, 0.0).astype(BF16)
        qt = q_ref[0, 0, rows, :].astype(F32).T
        dim = lax.broadcasted_iota(jnp.int32, qt.shape, 0)
        qs_scr[c, :, :tq] = jnp.where(dim < HEAD_DIM, qt, 0.0).astype(BF16)
        qs_scr[c, :, tq:] = jnp.where(dim >= HEAD_DIM, qt, 0.0).astype(BF16)
    m_scr[...] = jnp.full_like(m_scr, MASK_VALUE)
    acc_scr[...] = jnp.zeros_like(acc_scr)

    def scores(qi, kb, s_scr, diagonal):
        start = pl.multiple_of(kb * tq, tq)
        k = k_ref[0, 0, pl.ds(start, tq), :]
        s = jnp.dot(k, qs_scr[qi], preferred_element_type=F32)
        s = s + bias_ref[...] if diagonal else s
        s_scr[...] = s
        bm_scr[0 if s_scr is sa_scr else 1, 0:1, :] = jnp.max(s, axis=0, keepdims=True)

    def update(qi, kb, s_scr):
        s = s_scr[...]
        m_prev = m_scr[qi, 0:1, :]
        m_new = jnp.maximum(m_prev, bm_scr[0 if s_scr is sa_scr else 1, 0:1, :])
        alpha = jnp.exp2(m_prev - m_new)
        p = jnp.exp2(s - m_new).astype(BF16)
        pv = jnp.dot(vt_scr[kb], p, preferred_element_type=F32)
        acc_scr[qi] = alpha * acc_scr[qi] + pv
        m_scr[qi, 0:1, :] = m_new

    def finish(qi):
        acc = acc_scr[qi]
        o_all = acc[:V_HEAD_DIM] * (1.0 / acc[V_HEAD_DIM:V_HEAD_DIM + 1])
        o = o_all[:, :tq] - lam_ref[:, 0:1] * o_all[:, tq:]
        o = o * lax.rsqrt(jnp.mean(o * o, axis=0, keepdims=True) + SUBLN_EPS)
        o = (o * g_ref[...]) * (1.0 - lam_init)
        rows = pl.ds(pl.multiple_of(qi * tq, tq), tq)
        o_ref[0, 0, rows, :] = (o.T * jax.nn.silu(z_ref[0, 0, rows, :].astype(F32))).astype(BF16)

    def walk(count, first, following, diagonal, unroll):
        if count == 0:
            return
        last_blk = nblk - 1
        scores(first[0], first[1], sa_scr, diagonal)

        def steps(n):
            def body(_, cur):
                for step in range(n):
                    here, there = (sa_scr, sb_scr) if step % 2 == 0 else (sb_scr, sa_scr)
                    nxt = following(*cur)
                    scores(jnp.minimum(nxt[0], last_blk), jnp.minimum(nxt[1], last_blk), there, diagonal)
                    update(cur[0], cur[1], here)
                    if diagonal:
                        finish(cur[0])
                    cur = nxt
                return cur
            return body

        cur = (jnp.int32(first[0]), jnp.int32(first[1]))
        cur = lax.fori_loop(0, count // unroll, steps(unroll), cur)
        for _ in range(count % unroll // 2):
            cur = steps(2)(0, cur)
        if count % 2:
            update(cur[0], cur[1], sa_scr)
            if diagonal:
                finish(cur[0])

    def below_next(qi, kb):
        last = kb == qi - 1
        return jnp.where(last, qi + 1, qi), jnp.where(last, 0, kb + 1)

    walk(nblk * (nblk - 1) // 2, (1, 0), below_next, False, WALK_UNROLL_BELOW)
    walk(nblk, (0, 0), lambda qi, kb: (qi + 1, kb + 1), True, WALK_UNROLL_DIAG)


def _attn(q, k, v, z2, lam_row, subln_g, lam_init):
    bsz, _, seq, _ = q.shape
    tq = BLK
    nblk = seq // tq
    whole = pl.BlockSpec((1, 1, seq, V_HEAD_DIM), lambda b, h: (b, h, 0, 0))
    return pl.pallas_call(
        functools.partial(_attn_kernel, lam_init=lam_init),
        grid=(bsz, N_HEADS),
        in_specs=[whole, whole, whole, whole, _const_spec((tq, 2 * tq)),
                  _const_spec((1, V_HEAD_DIM)), _const_spec((V_HEAD_DIM, 1))],
        out_specs=whole,
        out_shape=jax.ShapeDtypeStruct(q.shape, BF16),
        scratch_shapes=[
            pltpu.VMEM((nblk, V_HEAD_DIM, 2 * tq), BF16),
            pltpu.VMEM((nblk, V_ROWS, tq), BF16),
            pltpu.VMEM((tq, 2 * tq), F32),
            pltpu.VMEM((tq, 2 * tq), F32),
            pltpu.VMEM((2, 8, 2 * tq), F32),
            pltpu.VMEM((nblk, 8, 2 * tq), F32),
            pltpu.VMEM((nblk, V_ROWS, 2 * tq), F32),
        ],
        compiler_params=_cparams(("parallel", "parallel")),
        name="diff_attn",
    )(q, k, v, z2, _causal_bias(), lam_row, subln_g)


def _final_kernel(o_ref, h_ref, w_ref, g_ref, out_ref):
    y = jnp.concatenate([o_ref[0, hd] for hd in range(N_HEADS)], axis=1)
    h = h_ref[0] + jnp.dot(y, w_ref[...], preferred_element_type=F32)
    out = h * lax.rsqrt(jnp.mean(h * h, axis=-1, keepdims=True) + NORM_EPS) * g_ref[...]
    for j in range(SSM_L):
        out_ref[0, :, j, :] = out[j * BLK_CHUNKS:(j + 1) * BLK_CHUNKS]


def _final(o, h1, out_w, g):
    bsz, seq, _ = h1.shape
    tok = pl.BlockSpec((1, BLK, D_MODEL), lambda b, i: (b, i, 0))
    return pl.pallas_call(
        _final_kernel,
        grid=(bsz, seq // BLK),
        in_specs=[pl.BlockSpec((1, N_HEADS, BLK, V_HEAD_DIM), lambda b, i: (b, 0, i, 0)), tok,
                  _const_spec((D_MODEL, D_MODEL)), _const_spec((1, D_MODEL))],
        out_specs=_NATURAL_TOK,
        out_shape=jax.ShapeDtypeStruct((bsz, seq // SSM_L, SSM_L, D_MODEL), F32),
        compiler_params=_cparams(("parallel", "parallel")),
        name="attn_out_final",
    )(o, h1, out_w, g).reshape(bsz, seq, D_MODEL)


def _rope_tables(seq):
    half = ROT_DIM // 2
    inv = ROPE_THETA ** (-jnp.arange(0, ROT_DIM, 2, dtype=F32) / ROT_DIM)
    row = jnp.arange(seq, dtype=jnp.int32)
    pos = (row // BLK) * BLK + _block_position(row % BLK)
    ang = pos.astype(F32)[:, None] * inv[None, :]
    cos, sin = jnp.cos(ang), jnp.sin(ang)
    ones = jnp.ones((seq, HEAD_DIM - ROT_DIM), F32)
    zeros = jnp.zeros((seq, HEAD_DIM - ROT_DIM), F32)
    zh = jnp.zeros((seq, half), F32)
    cos_h = jnp.concatenate([cos, cos, ones], axis=1)
    lo_h = jnp.concatenate([-sin, zh, zeros], axis=1)
    hi_h = jnp.concatenate([zh, sin, zeros], axis=1)
    rep = LANES // HEAD_DIM
    return jnp.tile(cos_h, (1, rep)), jnp.tile(lo_h, (1, rep)), jnp.tile(hi_h, (1, rep))


def _pick(n, want):
    r = min(n, want)
    assert n % r == 0, (n, want)
    return r


def kernel(x, a_norm_g, a_in_w, a_lambda_re, a_lambda_im, a_log_dt, a_b_re, a_b_im, a_c_re, a_c_im, a_d, a_glu_w, a_glu_b, a_out_w, kv_norm_g, kv_w, b_norm_g, b_in_w, b_lambda_q1, b_lambda_k1, b_lambda_q2, b_lambda_k2, b_subln_g, b_out_w, final_norm_g):
    bsz, seq, d = x.shape
    assert d == D_MODEL and seq % BLK == 0
    nchunk = seq // SSM_L
    x = x.reshape(bsz, nchunk, SSM_L, D_MODEL)

    u, z = _inproj(x, a_norm_g[0][None], a_in_w[0].astype(BF16))
    a_c, c_c, k_c, a_l = _ssm_tables(a_lambda_re[0], a_lambda_im[0], a_log_dt[0], a_b_re[0],
                                     a_b_im[0], a_c_re[0], a_c_im[0], a_d[0])
    y = _ssm(u, a_c, c_c, k_c, a_l, _pick(nchunk, SCAN_ROWS))

    cos_t, slo_t, shi_t = _rope_tables(seq)
    h1, q, k, v, z2 = _mid(y, z, x, cos_t, slo_t, shi_t,
                           a_glu_w[0].astype(BF16), a_glu_b[0][None], a_out_w[0].astype(BF16),
                           kv_norm_g[None], kv_w.astype(BF16),
                           b_norm_g[0][None], b_in_w[0].astype(BF16))

    lam_init = 0.8 - 0.6 * math.exp(-0.3 * N_A_LAYERS)
    lam = (jnp.exp(jnp.sum(b_lambda_q1[0] * b_lambda_k1[0]))
           - jnp.exp(jnp.sum(b_lambda_q2[0] * b_lambda_k2[0])) + lam_init)
    lam_row = jnp.full((1, V_HEAD_DIM), lam, F32)
    o = _attn(q, k, v, z2, lam_row, b_subln_g[0][:, None], lam_init)

    return _final(o, h1, b_out_w[0].astype(BF16), final_norm_g[None])
```
